```python
import math, functools
import jax, jax.numpy as jnp
from jax import lax
import numpy as np

D_MODEL = 1024
BATCH = 16
SEQ = 4096
DEPTH = 1
DEC_BATCH = 128
DEC_SEQ = 8
PAST_LEN = 8192
PAGE_SIZE = 128

HG_HEADS = 4
HG_DK = 128
HG_DV = 128
HG_W = HG_HEADS * HG_DK
CHUNK = 64
ATT_HEADS = 8
ATT_HD = 64
ATT_W = ATT_HEADS * ATT_HD
IDX_HEADS = 8
IDX_DIM = 64
TOPK_MAX = 256
TOPK_FRAC = 4
QBLOCK = 64
N_BRANCH = 2
BRANCH_W = 512
D_FF = -(-8 * D_MODEL // (3 * 256)) * 256
EPS = 1e-6
IN_SIZES = (HG_W, HG_W, HG_W, HG_W, ATT_W, ATT_W, ATT_W, IDX_HEADS * IDX_DIM, IDX_DIM, IDX_HEADS, N_BRANCH * D_MODEL)
IN_COLS = sum(IN_SIZES)

kernel_name = 'hgrn2_dsa_gated_hybrid_step'


def rms_norm(x, g):
    xf = x.astype(jnp.float32)
    y = xf * lax.rsqrt(jnp.mean(xf * xf, axis=-1, keepdims=True) + EPS)
    return (y * g.astype(jnp.float32)).astype(x.dtype)


def layer_norm(x, g, b):
    xf = x.astype(jnp.float32)
    mu = jnp.mean(xf, axis=-1, keepdims=True)
    var = jnp.mean(jnp.square(xf - mu), axis=-1, keepdims=True)
    y = (xf - mu) * lax.rsqrt(var + EPS)
    return (y * g.astype(jnp.float32) + b.astype(jnp.float32)).astype(x.dtype)


def split_cols(h):
    outs, start = [], 0
    for n in IN_SIZES:
        outs.append(h[..., start:start + n])
        start += n
    return outs


def hgrn2_recurrence(q, log_f, k, v, s0):
    B, T, H, DK = q.shape
    DV = v.shape[-1]
    C = math.gcd(T, CHUNK)
    n = T // C
    f32 = jnp.float32

    def to_chunks(a):
        return a.astype(f32).reshape(B, n, C, *a.shape[2:]).swapaxes(0, 1)

    tri = jnp.tril(jnp.ones((C, C), dtype=bool))[None, :, :, None, None]

    def step(S, inp):
        qc, lfc, kc, vc = inp
        b = jnp.cumsum(lfc, axis=1)
        o_inter = jnp.einsum('bchk,bhkv->bchv', qc * jnp.exp(b), S)
        diff = b[:, :, None] - b[:, None, :]
        decay = jnp.exp(jnp.where(tri, diff, -jnp.inf))
        a = jnp.einsum('bthk,btshk->btsh', qc, decay * kc[:, None])
        o_intra = jnp.einsum('btsh,bshv->bthv', a, vc)
        b_last = b[:, -1]
        S_new = jnp.exp(b_last)[..., None] * S + jnp.einsum(
            'bshk,bshv->bhkv', kc * jnp.exp(b_last[:, None] - b), vc)
        return S_new, o_inter + o_intra

    S_fin, o = lax.scan(step, s0.astype(f32), (to_chunks(q), to_chunks(log_f), to_chunks(k), to_chunks(v)))
    o = o.swapaxes(0, 1).reshape(B, T, H, DV)
    return o, S_fin


def indexer_scores(qi, ki, wi):
    dots = jnp.einsum('bqhd,bsd->bqhs', qi.astype(jnp.float32), ki.astype(jnp.float32)) * (IDX_DIM ** -0.5)
    return jnp.einsum('bqh,bqhs->bqs', wi.astype(jnp.float32), jax.nn.relu(dots))


def sparse_attend(q, k_sel, v_sel, valid):
    logits = jnp.einsum('bqhd,bqkhd->bqhk', q.astype(jnp.float32), k_sel.astype(jnp.float32)) * (ATT_HD ** -0.5)
    logits = jnp.where(valid[:, :, None, :], logits, -jnp.inf)
    p = jax.nn.softmax(logits, axis=-1)
    return jnp.einsum('bqhk,bqkhd->bqhd', p.astype(v_sel.dtype), v_sel)


gather_rows = jax.vmap(lambda rows, idx: rows[idx])


def prompt_sparse_attention(q, k, v, qi, ki, wi):
    B, T = q.shape[:2]
    topk = min(TOPK_MAX, T // TOPK_FRAC)
    qb = math.gcd(T, QBLOCK)
    nb = T // qb
    kpos = jnp.arange(T)

    def blocks(a):
        return a.reshape(B, nb, qb, *a.shape[2:]).swapaxes(0, 1)

    def one_block(args):
        q_b, qi_b, wi_b, start = args
        qpos = start + jnp.arange(qb)
        score = indexer_scores(qi_b, ki, wi_b)
        score = jnp.where(kpos[None, None, :] <= qpos[None, :, None], score, -jnp.inf)
        _, sel = lax.top_k(score, topk)
        valid = sel <= qpos[None, :, None]
        return sparse_attend(q_b, gather_rows(k, sel), gather_rows(v, sel), valid)

    starts = jnp.arange(nb, dtype=jnp.int32) * qb
    o = lax.map(one_block, (blocks(q), blocks(qi), blocks(wi), starts))
    return o.swapaxes(0, 1).reshape(B, T, ATT_W)


def sample_sparse_attention(q, k, v, qi, ki, wi, cache_k, cache_v, cache_kidx, page_table):
    B, T = q.shape[:2]
    past = page_table.shape[1] * PAGE_SIZE
    L = past + T
    topk = min(TOPK_MAX, L // TOPK_FRAC)
    ki_past = cache_kidx[page_table].reshape(B, past, IDX_DIM).astype(ki.dtype)
    keys = jnp.concatenate([ki_past, ki], axis=1)
    qpos = past + jnp.arange(T)
    kpos = jnp.arange(L)
    score = indexer_scores(qi, keys, wi)
    score = jnp.where(kpos[None, None, :] <= qpos[None, :, None], score, -jnp.inf)
    _, sel = lax.top_k(score, topk)
    valid = sel <= qpos[None, :, None]
    is_past = (sel < past)[..., None, None]
    phys = jax.vmap(lambda pt, s: pt[s])(page_table, jnp.minimum(sel, past - 1) // PAGE_SIZE)
    off = sel % PAGE_SIZE
    new_idx = jnp.clip(sel - past, 0, T - 1)
    k_sel = jnp.where(is_past, cache_k[phys, off].astype(k.dtype), gather_rows(k, new_idx))
    v_sel = jnp.where(is_past, cache_v[phys, off].astype(v.dtype), gather_rows(v, new_idx))
    o = sparse_attend(q, k_sel, v_sel, valid)
    return o.reshape(B, T, ATT_W)


def hybrid_mixer(xn, w_in, lb, hg_onorm_g, idx_kn_g, idx_kn_b, w_branch, w_out, s0, sparse_attention):
    B, T, _ = xn.shape
    hq, hf, hi, hg, aq, ak, av, iq, ik, iw, gates = split_cols(xn @ w_in)
    lbh = lb.reshape(HG_HEADS, HG_DK)
    f = lbh + (1.0 - lbh) * jax.nn.sigmoid(hf.reshape(B, T, HG_HEADS, HG_DK).astype(jnp.float32))
    o_a, s_fin = hgrn2_recurrence(hq.reshape(B, T, HG_HEADS, HG_DK), jnp.log(f), 1.0 - f,
                                  hi.reshape(B, T, HG_HEADS, HG_DV), s0)
    o_a = rms_norm(o_a, hg_onorm_g) * jax.nn.silu(hg.reshape(B, T, HG_HEADS, HG_DV).astype(jnp.float32))
    o_a = o_a.reshape(B, T, HG_W).astype(xn.dtype)
    q = aq.reshape(B, T, ATT_HEADS, ATT_HD)
    k = ak.reshape(B, T, ATT_HEADS, ATT_HD)
    v = av.reshape(B, T, ATT_HEADS, ATT_HD)
    qi = iq.reshape(B, T, IDX_HEADS, IDX_DIM)
    ki = layer_norm(ik, idx_kn_g, idx_kn_b)
    wi = iw * (IDX_HEADS ** -0.5)
    o_b = sparse_attention(q, k, v, qi, ki, wi)
    branches = jnp.stack([o_a, o_b.astype(xn.dtype)], axis=2)
    g = jax.nn.sigmoid(gates.reshape(B, T, N_BRANCH, D_MODEL).astype(jnp.float32)).astype(xn.dtype)
    merged = jnp.sum(jnp.einsum('btnc,ncd->btnd', branches, w_branch) * g, axis=2)
    return merged @ w_out, s_fin, k, v, ki


def swiglu(x, w_ffn_in, w_ffn_out):
    a, b = jnp.split(x @ w_ffn_in, 2, axis=-1)
    return (jax.nn.silu(a) * b) @ w_ffn_out


def setup_inputs(seed: int = 0) -> dict:
    key = jax.random.key(seed)
    ks = jax.random.split(key, 20)
    n_pages = PAST_LEN // PAGE_SIZE
    n_used = DEC_BATCH * n_pages
    n_pool = n_used + n_used // 4
    nrm = jax.random.normal
    f32 = jnp.float32
    perm = jax.random.permutation(ks[6], n_pool)[:n_used]
    return {
        'x_prompt': nrm(ks[0], (BATCH, SEQ, D_MODEL), f32),
        'x_sample': nrm(ks[1], (DEC_BATCH, DEC_SEQ, D_MODEL), f32),
        'cache_k': nrm(ks[2], (DEPTH, n_pool, PAGE_SIZE, ATT_HEADS, ATT_HD), f32),
        'cache_v': nrm(ks[3], (DEPTH, n_pool, PAGE_SIZE, ATT_HEADS, ATT_HD), f32),
        'cache_kidx': nrm(ks[4], (DEPTH, n_pool, PAGE_SIZE, IDX_DIM), f32),
        'state_hgrn': 0.5 * nrm(ks[5], (DEPTH, DEC_BATCH, HG_HEADS, HG_DK, HG_DV), f32),
        'page_table': perm.reshape(DEC_BATCH, n_pages).astype(jnp.int32),
        'norm1_g': 1.0 + 0.01 * nrm(ks[7], (DEPTH, D_MODEL), f32),
        'w_in': nrm(ks[8], (DEPTH, D_MODEL, IN_COLS), f32) * D_MODEL ** -0.5,
        'hg_lb_raw': nrm(ks[9], (DEPTH + 1, HG_W), f32),
        'hg_onorm_g': 1.0 + 0.01 * nrm(ks[10], (DEPTH, HG_DV), f32),
        'idx_knorm_g': 1.0 + 0.01 * nrm(ks[11], (DEPTH, IDX_DIM), f32),
        'idx_knorm_b': 0.01 * nrm(ks[12], (DEPTH, IDX_DIM), f32),
        'w_branch': nrm(ks[13], (DEPTH, N_BRANCH, BRANCH_W, D_MODEL), f32) * BRANCH_W ** -0.5,
        'w_out': nrm(ks[14], (DEPTH, D_MODEL, D_MODEL), f32) * D_MODEL ** -0.5,
        'norm2_g': 1.0 + 0.01 * nrm(ks[15], (DEPTH, D_MODEL), f32),
        'w_ffn_in': nrm(ks[16], (DEPTH, D_MODEL, 2 * D_FF), f32) * D_MODEL ** -0.5,
        'w_ffn_out': nrm(ks[17], (DEPTH, D_FF, D_MODEL), f32) * D_FF ** -0.5,
        'final_g': 1.0 + 0.01 * nrm(ks[18], (D_MODEL,), f32),
    }


def reference(x_prompt, x_sample, cache_k, cache_v, cache_kidx, state_hgrn, page_table,
              norm1_g, w_in, hg_lb_raw, hg_onorm_g, idx_knorm_g, idx_knorm_b, w_branch, w_out,
              norm2_g, w_ffn_in, w_ffn_out, final_g):
    lb_all = jnp.cumsum(jax.nn.softmax(hg_lb_raw.astype(jnp.float32), axis=0), axis=0)
    xp, xs = x_prompt, x_sample
    kp_l, vp_l, kip_l, sp_l, ks_l, vs_l, kis_l, ss_l = [], [], [], [], [], [], [], []
    for l in range(DEPTH):
        common = (w_in[l], lb_all[l], hg_onorm_g[l], idx_knorm_g[l], idx_knorm_b[l], w_branch[l], w_out[l])
        s0p = jnp.zeros((xp.shape[0], HG_HEADS, HG_DK, HG_DV), jnp.float32)
        mix, s_p, k_p, v_p, ki_p = hybrid_mixer(rms_norm(xp, norm1_g[l]), *common, s0p, prompt_sparse_attention)
        xp = xp + mix
        xp = xp + swiglu(rms_norm(xp, norm2_g[l]), w_ffn_in[l], w_ffn_out[l])
        attn_s = functools.partial(sample_sparse_attention, cache_k=cache_k[l], cache_v=cache_v[l],
                                   cache_kidx=cache_kidx[l], page_table=page_table)
        mix, s_s, k_s, v_s, ki_s = hybrid_mixer(rms_norm(xs, norm1_g[l]), *common, state_hgrn[l], attn_s)
        xs = xs + mix
        xs = xs + swiglu(rms_norm(xs, norm2_g[l]), w_ffn_in[l], w_ffn_out[l])
        kp_l.append(k_p); vp_l.append(v_p); kip_l.append(ki_p); sp_l.append(s_p)
        ks_l.append(k_s); vs_l.append(v_s); kis_l.append(ki_s); ss_l.append(s_s)
    y_prompt = rms_norm(xp, final_g)
    y_sample = rms_norm(xs, final_g)
    return (y_prompt, y_sample,
            jnp.stack(kp_l), jnp.stack(vp_l), jnp.stack(kip_l), jnp.stack(sp_l),
            jnp.stack(ks_l), jnp.stack(vs_l), jnp.stack(kis_l), jnp.stack(ss_l))
```

```python
import functools
import math

import numpy as np
import jax
import jax.numpy as jnp
from jax import lax
from jax.experimental import pallas as pl
from jax.experimental.pallas import tpu as pltpu

F32 = jnp.float32
BF16 = jnp.bfloat16

D_MODEL = 1024
HG_HEADS, HG_DK, HG_DV = 4, 128, 128
HG_W = HG_HEADS * HG_DK
HG_CHUNK = 64
ATT_HEADS, ATT_HD = 8, 64
ATT_W = ATT_HEADS * ATT_HD
IDX_HEADS, IDX_DIM = 8, 64
IDX_W = IDX_HEADS * IDX_DIM
TOPK_MAX, TOPK_FRAC = 256, 4
PAGE_SIZE = 128
N_BRANCH = 2
EPS = 1e-6
LANES = 128
VMEM_LIMIT = 56 * 1024 * 1024

_MAIN_W = 4 * HG_W + 3 * ATT_W + IDX_W
_SMALL_W = IDX_DIM + IDX_HEADS
_GATE_W = N_BRANCH * D_MODEL

NEG_BIG = -1e30
INT_MIN = -(2 ** 31)


def _const_spec(shape):
    nd = len(shape)
    return pl.BlockSpec(shape, lambda *_: (0,) * nd)


def _rms(xf, g):
    return xf * lax.rsqrt(jnp.mean(xf * xf, axis=-1, keepdims=True) + EPS) * g


def _sigmoid(x):
    return 1.0 / (1.0 + jnp.exp(-x))


def _inproj_kernel(x_ref, g_ref, wm_ref, ws_ref, wg_ref, kng_ref, knb_ref,
                   h4_ref, aq_ref, k_ref, v_ref, vb_ref, iq_ref, small_ref, gate_ref,
                   *maybe_t_refs):
    xn = _rms(x_ref[...], g_ref[...]).astype(BF16)

    def piece(j):
        return jnp.dot(xn, wm_ref[:, j * HG_W:(j + 1) * HG_W], preferred_element_type=F32)

    for j in range(4):
        h4_ref[:, j * HG_W:(j + 1) * HG_W] = piece(j)
    aq_ref[...] = piece(4).astype(BF16)
    k = piece(5)
    k_ref[...] = k
    v = piece(6)
    v_ref[...] = v
    vb_ref[...] = v.astype(BF16)
    iq_ref[...] = piece(7).astype(BF16)

    s = jnp.dot(xn, ws_ref[...], preferred_element_type=F32)
    lane = lax.broadcasted_iota(jnp.int32, s.shape, 1)
    is_key = lane < IDX_DIM
    sk = jnp.where(is_key, s, 0.0)
    mu = jnp.sum(sk, axis=-1, keepdims=True) * (1.0 / IDX_DIM)
    cen = jnp.where(is_key, s - mu, 0.0)
    var = jnp.sum(cen * cen, axis=-1, keepdims=True) * (1.0 / IDX_DIM)
    ki = cen * lax.rsqrt(var + EPS) * kng_ref[...] + knb_ref[...]
    wi = (s * (IDX_HEADS ** -0.5)) * (IDX_DIM ** -0.5)
    small = jnp.where(is_key, ki, jnp.where(lane < _SMALL_W, wi, 0.0))
    small_ref[...] = small

    gate_ref[...] = _sigmoid(jnp.dot(xn, wg_ref[...], preferred_element_type=F32)).astype(BF16)

    if maybe_t_refs:
        kT_ref, kiT_ref = maybe_t_refs
        kT_ref[0] = k.T.astype(BF16)
        kiT = small.T[:IDX_DIM].astype(BF16)
        kiT_ref[0, :IDX_DIM, :] = kiT
        kiT_ref[0, IDX_DIM:, :] = kiT


def _inproj(x2, norm_g, wm, ws, wg, kng, knb, *, seq, tm, transposed):
    n = x2.shape[0]
    assert n % tm == 0 and (not transposed or seq % tm == 0)
    row = lambda w: pl.BlockSpec((tm, w), lambda i: (i, 0))
    out_shape = [
        jax.ShapeDtypeStruct((n, 4 * HG_W), F32),
        jax.ShapeDtypeStruct((n, ATT_W), BF16),
        jax.ShapeDtypeStruct((n, ATT_W), F32),
        jax.ShapeDtypeStruct((n, ATT_W), F32),
        jax.ShapeDtypeStruct((n, ATT_W), BF16),
        jax.ShapeDtypeStruct((n, IDX_W), BF16),
        jax.ShapeDtypeStruct((n, LANES), F32),
        jax.ShapeDtypeStruct((n, _GATE_W), BF16),
    ]
    out_specs = [row(4 * HG_W), row(ATT_W), row(ATT_W), row(ATT_W), row(ATT_W), row(IDX_W),
                 row(LANES), row(_GATE_W)]
    if transposed:
        per = seq // tm
        out_shape += [jax.ShapeDtypeStruct((n // seq, ATT_W, seq), BF16),
                      jax.ShapeDtypeStruct((n // seq, 2 * IDX_DIM, seq), BF16)]
        out_specs += [pl.BlockSpec((1, ATT_W, tm), lambda i: (i // per, 0, i % per)),
                      pl.BlockSpec((1, 2 * IDX_DIM, tm), lambda i: (i // per, 0, i % per))]
    return pl.pallas_call(
        _inproj_kernel,
        grid=(n // tm,),
        in_specs=[row(D_MODEL), _const_spec((1, D_MODEL)), _const_spec(wm.shape), _const_spec(ws.shape),
                  _const_spec(wg.shape), _const_spec((1, LANES)), _const_spec((1, LANES))],
        out_specs=out_specs,
        out_shape=out_shape,
        compiler_params=pltpu.CompilerParams(dimension_semantics=("arbitrary",),
                                             vmem_limit_bytes=VMEM_LIMIT),
        name="inproj",
    )(x2, norm_g, wm, ws, wg, kng, knb)


def _hgrn_tables(c):
    levels = int(math.log2(c))
    assert 2 ** levels == c
    t = np.arange(c)[:, None]
    u = np.arange(c)[None, :]
    sums = [(u <= t), (u > t)]
    masks = [np.eye(c, dtype=bool)]
    for lv in range(levels):
        m = 2 ** lv
        mid = (t // (2 * m)) * (2 * m) + m
        upper = (t % (2 * m)) >= m
        sums.append(upper & (u >= mid) & (u <= t))
        sums.append((~upper) & (u > t) & (u < mid))
        same = (t // (2 * m)) == (u // (2 * m))
        masks.append(same & upper & ((u % (2 * m)) < m))
    return (np.concatenate(sums, 0).astype(np.float32), np.stack(masks).astype(np.float32), levels)


def _hgrn_kernel(h4_ref, s0_ref, lbraw_ref, ong_ref, sums_ref, masks_ref, o_ref, sfin_ref, st_ref,
                 *, c, valid, levels, layer):
    ci = pl.program_id(1)

    @pl.when(ci == 0)
    def _():
        for h in range(HG_HEADS):
            st_ref[h] = s0_ref[0, h].T

    lbraw = lbraw_ref[...]
    e = jnp.exp(lbraw - jnp.max(lbraw, axis=0, keepdims=True))
    sm = e / jnp.sum(e, axis=0, keepdims=True)
    lb = jnp.sum(sm[:layer + 1], axis=0, keepdims=True)

    q = h4_ref[:, 0:HG_W]
    f = lb + (1.0 - lb) * _sigmoid(h4_ref[:, HG_W:2 * HG_W])
    logf = jnp.log(f)
    kk = 1.0 - f
    if valid < c:
        live = lax.broadcasted_iota(jnp.int32, f.shape, 0) < valid
        logf = jnp.where(live, logf, 0.0)
        kk = jnp.where(live, kk, 0.0)
    v = h4_ref[:, 2 * HG_W:3 * HG_W]

    p0 = logf.astype(BF16)
    r0 = logf - p0.astype(F32)
    p1 = r0.astype(BF16)
    p2 = (r0 - p1.astype(F32)).astype(BF16)
    sums = sums_ref[...]
    ex = (jnp.dot(sums, p0, preferred_element_type=F32) + jnp.dot(sums, p1, preferred_element_type=F32)
          + jnp.dot(sums, p2, preferred_element_type=F32))
    nt = (((1,), (1,)), ((), ()))
    tn = (((0,), (0,)), ((), ()))
    for h in range(HG_HEADS):
        hs = slice(h * HG_DK, (h + 1) * HG_DK)
        qh, kh, vh = q[:, hs], kk[:, hs], v[:, hs].astype(BF16)
        a = masks_ref[0] * lax.dot_general(qh.astype(BF16), kh.astype(BF16), nt, preferred_element_type=F32)
        for lv in range(levels):
            eq = ex[(2 + 2 * lv) * c:(3 + 2 * lv) * c, hs]
            ek = ex[(3 + 2 * lv) * c:(4 + 2 * lv) * c, hs]
            ql = (qh * jnp.exp(eq)).astype(BF16)
            kl = (kh * jnp.exp(ek)).astype(BF16)
            a = a + masks_ref[lv + 1] * lax.dot_general(ql, kl, nt, preferred_element_type=F32)
        st = st_ref[h]
        qi = (qh * jnp.exp(ex[0:c, hs])).astype(BF16)
        o = (jnp.dot(a.astype(BF16), vh, preferred_element_type=F32)
             + lax.dot_general(qi, st.astype(BF16), nt, preferred_element_type=F32))
        ke = (kh * jnp.exp(ex[c:2 * c, hs])).astype(BF16)
        st_ref[h] = st * jnp.exp(ex[c - 1:c, hs]) + lax.dot_general(vh, ke, tn, preferred_element_type=F32)
        on = _rms(o, ong_ref[...])
        g = h4_ref[:, 3 * HG_W + h * HG_DV:3 * HG_W + (h + 1) * HG_DV]
        o_ref[:, hs] = (on * (g * _sigmoid(g))).astype(BF16)

    @pl.when(ci == pl.num_programs(1) - 1)
    def _():
        for h in range(HG_HEADS):
            sfin_ref[0, h] = st_ref[h].T


def _hgrn(h4, s0, lbraw, ong, *, seq, chunk, valid, layer):
    n = h4.shape[0]
    b = n // seq
    c = chunk
    nc = seq // c
    assert seq % c == 0 and (valid == c or nc == 1)
    sums, masks, levels = _hgrn_tables(c)
    sums = jnp.asarray(sums, BF16)
    masks = jnp.asarray(masks, F32)
    return pl.pallas_call(
        functools.partial(_hgrn_kernel, c=c, valid=valid, levels=levels, layer=layer),
        grid=(b, nc),
        in_specs=[pl.BlockSpec((c, 4 * HG_W), lambda i, j: (i * nc + j, 0)),
                  pl.BlockSpec((1, HG_HEADS, HG_DK, HG_DV), lambda i, j: (i, 0, 0, 0)),
                  _const_spec(lbraw.shape), _const_spec((1, HG_DV)), _const_spec(sums.shape),
                  _const_spec(masks.shape)],
        out_specs=[pl.BlockSpec((c, HG_W), lambda i, j: (i * nc + j, 0)),
                   pl.BlockSpec((1, HG_HEADS, HG_DK, HG_DV), lambda i, j: (i, 0, 0, 0))],
        out_shape=[jax.ShapeDtypeStruct((n, HG_W), BF16),
                   jax.ShapeDtypeStruct((b, HG_HEADS, HG_DK, HG_DV), F32)],
        scratch_shapes=[pltpu.VMEM((HG_HEADS, HG_DV, HG_DK), F32)],
        compiler_params=pltpu.CompilerParams(dimension_semantics=("arbitrary", "arbitrary"),
                                             vmem_limit_bytes=VMEM_LIMIT),
        name="hgrn",
    )(h4, s0, lbraw, ong, sums, masks)


_KEY_NEG_INF = int(np.array([-np.inf], np.float32).view(np.int32)[0]) ^ 0x7FFFFFFF
_INT_MAX = 2 ** 31 - 1


def _sort_key(s):
    s = jnp.where(s == 0.0, 0.0, s)
    bits = pltpu.bitcast(s, jnp.int32)
    return jnp.where(bits < 0, bits ^ 0x7FFFFFFF, bits)


def _fold_lanes(x):
    acc = x[:, :LANES]
    for j in range(1, x.shape[1] // LANES):
        acc = acc + x[:, j * LANES:(j + 1) * LANES]
    return acc


def _select_bias(key_ref, bias_ref, cut_ref, rows, ntiles, tk, topk, idx_bits):
    rb = rows.stop - rows.start

    def count(indicator):
        def body(kt, acc):
            k0 = pl.multiple_of(kt * tk, tk)
            kk = key_ref[rows, pl.ds(k0, tk)]
            col = k0 + lax.broadcasted_iota(jnp.int32, (rb, tk), 1)
            return acc + _fold_lanes(indicator(kk, col))
        acc = lax.fori_loop(0, ntiles, body, jnp.zeros((rb, LANES), F32))
        return jnp.sum(acc, axis=-1, keepdims=True)

    def bit_step(i, t):
        cand = t + jnp.left_shift(jnp.int32(1), 31 - i)
        return jnp.where(count(lambda kk, col: jnp.where(kk >= cand, 1.0, 0.0)) >= topk, cand, t)

    t = lax.fori_loop(0, 32, bit_step, jnp.full((rb, 1), INT_MIN, jnp.int32))
    n_gt = count(lambda kk, col: jnp.where(kk > t, 1.0, 0.0))
    n_ge = count(lambda kk, col: jnp.where(kk >= t, 1.0, 0.0))
    need = topk - n_gt
    has_cut = jnp.logical_and(n_ge > topk, t != _KEY_NEG_INF)
    cut_ref[rows, :] = jnp.where(t == _KEY_NEG_INF, -1, _INT_MAX) + jnp.zeros((rb, LANES), jnp.int32)

    @pl.when(jnp.max(has_cut.astype(jnp.int32)) > 0)
    def _():
        def idx_step(i, x):
            cand = x + jnp.left_shift(jnp.int32(1), idx_bits - 1 - i)
            below = count(lambda kk, col: jnp.where(kk == t, jnp.where(col < cand, 1.0, 0.0), 0.0))
            return jnp.where(below < need, cand, x)
        x = lax.fori_loop(0, idx_bits, idx_step, jnp.zeros((rb, 1), jnp.int32))
        cut_ref[rows, :] = jnp.where(has_cut, x, cut_ref[rows, 0:1]) + jnp.zeros((rb, LANES), jnp.int32)

    cut = cut_ref[rows, 0:1]

    def write(kt, carry):
        k0 = pl.multiple_of(kt * tk, tk)
        kk = key_ref[rows, pl.ds(k0, tk)]
        col = k0 + lax.broadcasted_iota(jnp.int32, (rb, tk), 1)
        tie = jnp.where(col <= cut, 0.0, NEG_BIG)
        bias_ref[rows, pl.ds(k0, tk)] = jnp.where(kk == t, tie, jnp.where(kk > t, 0.0, NEG_BIG))
        return carry
    lax.fori_loop(0, ntiles, write, 0)


def _pattn_kernel(q_ref, iq_ref, small_ref, kT_ref, vb_ref, kiT_ref, o_ref,
                  qm_scr, iqm_scr, key_scr, bias_scr, cut_scr, m_scr, l_scr, acc_scr,
                  *, tq, tk, rb, topk, idx_bits):
    qb = pl.program_id(1)
    q0 = qb * tq
    ntiles = (q0 + tq) // tk
    lane = lax.broadcasted_iota(jnp.int32, (tq, LANES), 1)
    for h in range(ATT_HEADS):
        ps = slice((h // 2) * LANES, (h // 2 + 1) * LANES)
        mine = (lane < ATT_HD) if h % 2 == 0 else (lane >= ATT_HD)
        qm_scr[h] = jnp.where(mine, q_ref[:, ps] * (ATT_HD ** -0.5), 0.0).astype(BF16)
        iqm_scr[h] = jnp.where(mine, iq_ref[:, ps], 0.0).astype(BF16)
    qpos = q0 + lax.broadcasted_iota(jnp.int32, (tq, 1), 0)

    def score_tile(kt, carry):
        k0 = pl.multiple_of(kt * tk, tk)
        kit = kiT_ref[0, :, pl.ds(k0, tk)]
        acc = jnp.zeros((tq, tk), F32)
        for h in range(IDX_HEADS):
            d = jnp.dot(iqm_scr[h], kit, preferred_element_type=F32)
            acc = acc + small_ref[:, IDX_DIM + h:IDX_DIM + h + 1] * jnp.maximum(d, 0.0)
        kpos = k0 + lax.broadcasted_iota(jnp.int32, (tq, tk), 1)
        key_scr[:, pl.ds(k0, tk)] = _sort_key(jnp.where(kpos <= qpos, acc, -jnp.inf))
        return carry
    lax.fori_loop(0, ntiles, score_tile, 0)

    for r in range(tq // rb):
        _select_bias(key_scr, bias_scr, cut_scr, slice(r * rb, (r + 1) * rb), ntiles, tk, topk, idx_bits)

    m_scr[...] = jnp.full(m_scr.shape, NEG_BIG, F32)
    l_scr[...] = jnp.zeros(l_scr.shape, F32)
    acc_scr[...] = jnp.zeros(acc_scr.shape, F32)

    def attend_tile(kt, carry):
        k0 = pl.multiple_of(kt * tk, tk)
        bias = bias_scr[:, pl.ds(k0, tk)]
        for h in range(ATT_HEADS):
            ps = slice((h // 2) * LANES, (h // 2 + 1) * LANES)
            lg = jnp.dot(qm_scr[h], kT_ref[0, ps, pl.ds(k0, tk)], preferred_element_type=F32) + bias
            m_old = m_scr[h]
            m_new = jnp.maximum(m_old, jnp.max(lg, axis=-1, keepdims=True))
            alpha = jnp.exp(m_old - m_new)
            p = jnp.exp(lg - m_new)
            l_scr[h] = alpha * l_scr[h] + jnp.sum(p, axis=-1, keepdims=True)
            acc_scr[h] = alpha * acc_scr[h] + jnp.dot(p.astype(BF16), vb_ref[pl.ds(k0, tk), ps],
                                                      preferred_element_type=F32)
            m_scr[h] = m_new
        return carry
    lax.fori_loop(0, ntiles, attend_tile, 0)

    for hp in range(ATT_HEADS // 2):
        even = acc_scr[2 * hp] / l_scr[2 * hp]
        odd = acc_scr[2 * hp + 1] / l_scr[2 * hp + 1]
        o_ref[:, hp * LANES:(hp + 1) * LANES] = jnp.where(lane < ATT_HD, even, odd).astype(BF16)


def _pattn(aq, iq, small, kT, vb, kiT, *, seq, tq, tk, rb):
    n = aq.shape[0]
    b = n // seq
    topk = min(TOPK_MAX, seq // TOPK_FRAC)
    assert seq % tq == 0 and tq % tk == 0 and tk >= topk and tk % LANES == 0 and tq % rb == 0
    nq = seq // tq
    row = lambda w: pl.BlockSpec((tq, w), lambda i, j: (i * nq + j, 0))
    return pl.pallas_call(
        functools.partial(_pattn_kernel, tq=tq, tk=tk, rb=rb, topk=topk, idx_bits=(seq - 1).bit_length()),
        grid=(b, nq),
        in_specs=[row(ATT_W), row(IDX_W), row(LANES),
                  pl.BlockSpec((1, ATT_W, seq), lambda i, j: (i, 0, 0)),
                  pl.BlockSpec((seq, ATT_W), lambda i, j: (i, 0)),
                  pl.BlockSpec((1, 2 * IDX_DIM, seq), lambda i, j: (i, 0, 0))],
        out_specs=row(ATT_W),
        out_shape=jax.ShapeDtypeStruct((n, ATT_W), BF16),
        scratch_shapes=[pltpu.VMEM((ATT_HEADS, tq, LANES), BF16), pltpu.VMEM((IDX_HEADS, tq, LANES), BF16),
                        pltpu.VMEM((tq, seq), jnp.int32), pltpu.VMEM((tq, seq), F32),
                        pltpu.VMEM((tq, LANES), jnp.int32),
                        pltpu.VMEM((ATT_HEADS, tq, 1), F32), pltpu.VMEM((ATT_HEADS, tq, 1), F32),
                        pltpu.VMEM((ATT_HEADS, tq, LANES), F32)],
        compiler_params=pltpu.CompilerParams(dimension_semantics=("arbitrary", "arbitrary"),
                                             vmem_limit_bytes=VMEM_LIMIT),
        name="prompt_attn",
    )(aq, iq, small, kT, vb, kiT)


def _sscore_kernel(pt_ref, iq_ref, w_ref, kinew_ref, *rest, pp, past, tnew, topk, idx_bits):
    page_refs, (bias_ref, key_scr, cut_scr) = rest[:pp], rest[pp:]
    j = pl.program_id(1)
    nt = (((1,), (1,)), ((), ()))
    iq = iq_ref[0]
    w = w_ref[0]

    def scores(keys_bf16):
        d = lax.dot_general(iq, keys_bf16, nt, preferred_element_type=F32)
        r = w * jnp.maximum(d, 0.0)
        acc = jnp.zeros((tnew, LANES), F32)
        for h in range(IDX_HEADS):
            acc = acc + r[h * tnew:(h + 1) * tnew]
        return acc

    for i in range(pp):
        k0 = pl.multiple_of((j * pp + i) * PAGE_SIZE, PAGE_SIZE)
        key_scr[:, pl.ds(k0, PAGE_SIZE)] = _sort_key(scores(page_refs[i][0].astype(BF16)))

    @pl.when(j == pl.num_programs(1) - 1)
    def _():
        s = scores(kinew_ref[0])
        qrow = lax.broadcasted_iota(jnp.int32, (tnew, LANES), 0)
        kcol = lax.broadcasted_iota(jnp.int32, (tnew, LANES), 1)
        key_scr[:, past:past + LANES] = _sort_key(jnp.where(kcol <= qrow, s, -jnp.inf))
        _select_bias(key_scr, bias_ref.at[0], cut_scr, slice(0, tnew), past // LANES + 1, LANES, topk, idx_bits)


def _sscore(page_table, iq_s, w_s, kinew, cache_kidx, *, pp):
    b, n_pages = page_table.shape
    tnew = iq_s.shape[1] // IDX_HEADS
    past = n_pages * PAGE_SIZE
    total = past + LANES
    topk = min(TOPK_MAX, (past + tnew) // TOPK_FRAC)
    assert n_pages % pp == 0 and tnew <= LANES and tnew % 8 == 0
    page_specs = [pl.BlockSpec((1, PAGE_SIZE, IDX_DIM), lambda i, j, pt, s=s: (pt[i, j * pp + s], 0, 0))
                  for s in range(pp)]
    grid_spec = pltpu.PrefetchScalarGridSpec(
        num_scalar_prefetch=1,
        grid=(b, n_pages // pp),
        in_specs=[pl.BlockSpec((1, IDX_HEADS * tnew, IDX_DIM), lambda i, j, pt: (i, 0, 0)),
                  pl.BlockSpec((1, IDX_HEADS * tnew, 1), lambda i, j, pt: (i, 0, 0)),
                  pl.BlockSpec((1, LANES, IDX_DIM), lambda i, j, pt: (i, 0, 0))] + page_specs,
        out_specs=pl.BlockSpec((1, tnew, total), lambda i, j, pt: (i, 0, 0)),
        scratch_shapes=[pltpu.VMEM((tnew, total), jnp.int32), pltpu.VMEM((tnew, LANES), jnp.int32)])
    return pl.pallas_call(
        functools.partial(_sscore_kernel, pp=pp, past=past, tnew=tnew, topk=topk,
                          idx_bits=(total - 1).bit_length()),
        grid_spec=grid_spec,
        out_shape=jax.ShapeDtypeStruct((b, tnew, total), F32),
        compiler_params=pltpu.CompilerParams(dimension_semantics=("arbitrary", "arbitrary"),
                                             vmem_limit_bytes=VMEM_LIMIT),
        name="sample_select",
    )(page_table, iq_s, w_s, kinew, *([cache_kidx] * pp))


def _sattn_kernel(pt_ref, qbd_ref, bias_ref, tail_ref, knew_ref, vnew_ref, *rest, pp, tnew):
    k_refs, v_refs = rest[:pp], rest[pp:2 * pp]
    o_ref, m_scr, l_scr, acc_scr = rest[2 * pp:]
    j = pl.program_id(1)
    nt = (((1,), (1,)), ((), ()))
    qbd = qbd_ref[0]

    @pl.when(j == 0)
    def _():
        m_scr[...] = jnp.full(m_scr.shape, NEG_BIG, F32)
        l_scr[...] = jnp.zeros(l_scr.shape, F32)
        acc_scr[...] = jnp.zeros(acc_scr.shape, F32)

    def update(kb, vb, bias):
        lg = (lax.dot_general(qbd, kb, nt, preferred_element_type=F32)
              + jnp.concatenate([bias] * ATT_HEADS, axis=0))
        m_old = m_scr[...]
        m_new = jnp.maximum(m_old, jnp.max(lg, axis=-1, keepdims=True))
        alpha = jnp.exp(m_old - m_new)
        p = jnp.exp(lg - m_new)
        l_scr[...] = alpha * l_scr[...] + jnp.sum(p, axis=-1, keepdims=True)
        acc_scr[...] = alpha * acc_scr[...] + jnp.dot(p.astype(BF16), vb, preferred_element_type=F32)
        m_scr[...] = m_new

    for s in range(pp):
        update(k_refs[s][0].astype(BF16), v_refs[s][0].astype(BF16),
               bias_ref[0, :, s * PAGE_SIZE:(s + 1) * PAGE_SIZE])

    @pl.when(j == pl.num_programs(1) - 1)
    def _():
        update(knew_ref[0], vnew_ref[0], tail_ref[0])
        res = acc_scr[...] / l_scr[...]
        lane = lax.broadcasted_iota(jnp.int32, (tnew, ATT_W), 1)
        out = jnp.zeros((tnew, ATT_W), F32)
        for h in range(ATT_HEADS):
            mine = jnp.logical_and(lane >= h * ATT_HD, lane < (h + 1) * ATT_HD)
            out = out + jnp.where(mine, res[h * tnew:(h + 1) * tnew], 0.0)
        o_ref[0] = out.astype(BF16)


def _sattn(page_table, qbd, bias, knew, vnew, cache_k, cache_v, *, pp):
    b, n_pages = page_table.shape
    tnew = bias.shape[1]
    rows = ATT_HEADS * tnew
    kv_specs = [pl.BlockSpec((1, PAGE_SIZE, ATT_W), lambda i, j, pt, s=s: (pt[i, j * pp + s], 0, 0))
                for s in range(pp)]
    grid_spec = pltpu.PrefetchScalarGridSpec(
        num_scalar_prefetch=1,
        grid=(b, n_pages // pp),
        in_specs=[pl.BlockSpec((1, rows, ATT_W), lambda i, j, pt: (i, 0, 0)),
                  pl.BlockSpec((1, tnew, pp * PAGE_SIZE), lambda i, j, pt: (i, 0, j)),
                  pl.BlockSpec((1, tnew, LANES), lambda i, j, pt: (i, 0, n_pages)),
                  pl.BlockSpec((1, LANES, ATT_W), lambda i, j, pt: (i, 0, 0)),
                  pl.BlockSpec((1, LANES, ATT_W), lambda i, j, pt: (i, 0, 0))] + kv_specs + kv_specs,
        out_specs=pl.BlockSpec((1, tnew, ATT_W), lambda i, j, pt: (i, 0, 0)),
        scratch_shapes=[pltpu.VMEM((rows, 1), F32), pltpu.VMEM((rows, 1), F32), pltpu.VMEM((rows, ATT_W), F32)])
    return pl.pallas_call(
        functools.partial(_sattn_kernel, pp=pp, tnew=tnew),
        grid_spec=grid_spec,
        out_shape=jax.ShapeDtypeStruct((b, tnew, ATT_W), BF16),
        compiler_params=pltpu.CompilerParams(dimension_semantics=("arbitrary", "arbitrary"),
                                             vmem_limit_bytes=VMEM_LIMIT),
        name="sample_attn",
    )(page_table, qbd, bias, bias, knew, vnew, *([cache_k] * pp), *([cache_v] * pp))


def _mixffn_kernel(x_ref, oa_ref, ob_ref, gate_ref, wb_ref, wo_ref, g2_ref, wfi_ref, wfo_ref, gf_ref, y_ref,
                   *, d_ff, ff_chunk, final):
    ga = gate_ref[:, :D_MODEL].astype(F32)
    gb = gate_ref[:, D_MODEL:].astype(F32)
    merged = (jnp.dot(oa_ref[...], wb_ref[0], preferred_element_type=F32) * ga
              + jnp.dot(ob_ref[...], wb_ref[1], preferred_element_type=F32) * gb)
    x1 = x_ref[...] + jnp.dot(merged.astype(BF16), wo_ref[...], preferred_element_type=F32)
    xn = _rms(x1, g2_ref[...]).astype(BF16)
    acc = x1
    for c0 in range(0, d_ff, ff_chunk):
        a = jnp.dot(xn, wfi_ref[:, c0:c0 + ff_chunk], preferred_element_type=F32)
        bgate = jnp.dot(xn, wfi_ref[:, d_ff + c0:d_ff + c0 + ff_chunk], preferred_element_type=F32)
        hact = (a * _sigmoid(a) * bgate).astype(BF16)
        acc = acc + jnp.dot(hact, wfo_ref[c0:c0 + ff_chunk, :], preferred_element_type=F32)
    y_ref[...] = _rms(acc, gf_ref[...]) if final else acc


def _mixffn(x2, oa, ob, gate, wb, wo, g2, wfi, wfo, gf, *, tm, final):
    n = x2.shape[0]
    d_ff = wfo.shape[0]
    ff_chunk = 2 * LANES if d_ff % (2 * LANES) == 0 else d_ff
    row = lambda w: pl.BlockSpec((tm, w), lambda i: (i, 0))
    return pl.pallas_call(
        functools.partial(_mixffn_kernel, d_ff=d_ff, ff_chunk=ff_chunk, final=final),
        grid=(n // tm,),
        in_specs=[row(D_MODEL), row(HG_W), row(ATT_W), row(_GATE_W), _const_spec(wb.shape), _const_spec(wo.shape),
                  _const_spec((1, D_MODEL)), _const_spec(wfi.shape), _const_spec(wfo.shape),
                  _const_spec((1, D_MODEL))],
        out_specs=row(D_MODEL),
        out_shape=jax.ShapeDtypeStruct((n, D_MODEL), F32),
        compiler_params=pltpu.CompilerParams(dimension_semantics=("arbitrary",),
                                             vmem_limit_bytes=VMEM_LIMIT),
        name="mix_ffn",
    )(x2, oa, ob, gate, wb, wo, g2, wfi, wfo, gf)


def _pick_tile(n, pref):
    t = min(n, pref)
    while n % t:
        t //= 2
    return t


def kernel(x_prompt, x_sample, cache_k, cache_v, cache_kidx, state_hgrn, page_table, norm1_g, w_in, hg_lb_raw,
           hg_onorm_g, idx_knorm_g, idx_knorm_b, w_branch, w_out, norm2_g, w_ffn_in, w_ffn_out, final_g):
    bp, tp, _ = x_prompt.shape
    bs, ts, _ = x_sample.shape
    depth = w_in.shape[0]
    n_pool = cache_k.shape[1]
    xp = x_prompt.reshape(bp * tp, D_MODEL)
    xs = x_sample.reshape(bs * ts, D_MODEL)
    row = lambda a: a.reshape(1, -1)
    pad_lanes = lambda a: jnp.pad(a, (0, LANES - a.shape[0])).reshape(1, LANES)
    outs = [[] for _ in range(8)]
    for l in range(depth):
        wm = w_in[l, :, :_MAIN_W].astype(BF16)
        ws = jnp.pad(w_in[l, :, _MAIN_W:_MAIN_W + _SMALL_W], ((0, 0), (0, LANES - _SMALL_W))).astype(BF16)
        wg = w_in[l, :, _MAIN_W + _SMALL_W:].astype(BF16)
        kng, knb = pad_lanes(idx_knorm_g[l]), pad_lanes(idx_knorm_b[l])
        proj = functools.partial(_inproj, norm_g=row(norm1_g[l]), wm=wm, ws=ws, wg=wg, kng=kng, knb=knb)
        mix = functools.partial(_mixffn, wb=w_branch[l].astype(BF16), wo=w_out[l].astype(BF16), g2=row(norm2_g[l]),
                                wfi=w_ffn_in[l].astype(BF16), wfo=w_ffn_out[l].astype(BF16), gf=row(final_g),
                                final=(l == depth - 1))
        hgrn = functools.partial(_hgrn, lbraw=hg_lb_raw, ong=row(hg_onorm_g[l]), layer=l)

        tm = _pick_tile(tp, 512)
        h4, aq, k, v, vb, iq, small, gate, kT, kiT = proj(xp, seq=tp, tm=tm, transposed=True)
        cp = math.gcd(tp, HG_CHUNK)
        oa, s_p = hgrn(h4, jnp.zeros((bp, HG_HEADS, HG_DK, HG_DV), F32), seq=tp, chunk=cp, valid=cp)
        tq = _pick_tile(tp, 256)
        ob = _pattn(aq, iq, small, kT, vb, kiT, seq=tp, tq=tq, tk=tq, rb=min(tq, 64))
        xp = mix(xp, oa, ob, gate, tm=tm)
        outs[0].append(k.reshape(bp, tp, ATT_HEADS, ATT_HD))
        outs[1].append(v.reshape(bp, tp, ATT_HEADS, ATT_HD))
        outs[2].append(small[:, :IDX_DIM].reshape(bp, tp, IDX_DIM))
        outs[3].append(s_p)

        ns = bs * ts
        tms = _pick_tile(ns, 512)
        h4, aq, k, v, vb, iq, small, gate = proj(xs, seq=ts, tm=tms, transposed=False)
        cs = 16
        assert ts <= cs
        h4p = jnp.pad(h4.reshape(bs, ts, 4 * HG_W), ((0, 0), (0, cs - ts), (0, 0))).reshape(bs * cs, 4 * HG_W)
        oa, s_s = hgrn(h4p, state_hgrn[l], seq=cs, chunk=cs, valid=ts)
        oa = oa.reshape(bs, cs, HG_W)[:, :ts].reshape(ns, HG_W)
        iq_s = iq.reshape(bs, ts, IDX_HEADS, IDX_DIM).transpose(0, 2, 1, 3).reshape(bs, IDX_HEADS * ts, IDX_DIM)
        w_s = small[:, IDX_DIM:_SMALL_W].reshape(bs, ts, IDX_HEADS).transpose(0, 2, 1).reshape(bs, IDX_HEADS * ts, 1)
        pad_rows = lambda a: jnp.pad(a.reshape(bs, ts, -1), ((0, 0), (0, LANES - ts), (0, 0))).astype(BF16)
        n_pages = page_table.shape[1]
        pp = _pick_tile(n_pages, 8)
        bias = _sscore(page_table, iq_s, w_s, pad_rows(small[:, :IDX_DIM]),
                       cache_kidx[l].reshape(n_pool, PAGE_SIZE, IDX_DIM), pp=pp)
        qh = (aq.astype(F32) * (ATT_HD ** -0.5)).reshape(bs, ts, ATT_HEADS, ATT_HD).transpose(0, 2, 1, 3)
        eye = jnp.eye(ATT_HEADS, dtype=F32)
        qbd = (qh[:, :, :, None, :] * eye[None, :, None, :, None]).reshape(bs, ATT_HEADS * ts, ATT_W).astype(BF16)
        ob = _sattn(page_table, qbd, bias, pad_rows(k), pad_rows(v), cache_k[l].reshape(n_pool, PAGE_SIZE, ATT_W),
                    cache_v[l].reshape(n_pool, PAGE_SIZE, ATT_W), pp=pp).reshape(ns, ATT_W)
        xs = mix(xs, oa, ob, gate, tm=tms)
        outs[4].append(k.reshape(bs, ts, ATT_HEADS, ATT_HD))
        outs[5].append(v.reshape(bs, ts, ATT_HEADS, ATT_HD))
        outs[6].append(small[:, :IDX_DIM].reshape(bs, ts, IDX_DIM))
        outs[7].append(s_s)
    return (xp.reshape(bp, tp, D_MODEL), xs.reshape(bs, ts, D_MODEL)) + tuple(jnp.stack(o) for o in outs)
```

```python
import functools
import math

import numpy as np
import jax
import jax.numpy as jnp
from jax import lax
from jax.experimental import pallas as pl
from jax.experimental.pallas import tpu as pltpu

F32 = jnp.float32
BF16 = jnp.bfloat16

D_MODEL = 1024
HG_HEADS, HG_DK, HG_DV = 4, 128, 128
HG_W = HG_HEADS * HG_DK
HG_CHUNK = 64
ATT_HEADS, ATT_HD = 8, 64
ATT_W = ATT_HEADS * ATT_HD
IDX_HEADS, IDX_DIM = 8, 64
IDX_W = IDX_HEADS * IDX_DIM
TOPK_MAX, TOPK_FRAC = 256, 4
PAGE_SIZE = 128
N_BRANCH = 2
EPS = 1e-6
LANES = 128
VMEM_LIMIT = 56 * 1024 * 1024

_MAIN_W = 4 * HG_W + 3 * ATT_W + IDX_W
_SMALL_W = IDX_DIM + IDX_HEADS
_GATE_W = N_BRANCH * D_MODEL

NEG_BIG = -1e30
INT_MIN = -(2 ** 31)


def _const_spec(shape):
    nd = len(shape)
    return pl.BlockSpec(shape, lambda *_: (0,) * nd)


def _rms(xf, g):
    return xf * lax.rsqrt(jnp.mean(xf * xf, axis=-1, keepdims=True) + EPS) * g


def _sigmoid(x):
    return 1.0 / (1.0 + jnp.exp(-x))


def _inproj_kernel(x_ref, g_ref, wm_ref, ws_ref, wg_ref, kng_ref, knb_ref,
                   h4_ref, aq_ref, k_ref, v_ref, vb_ref, iq_ref, small_ref, gate_ref,
                   *maybe_t_refs):
    xn = _rms(x_ref[...], g_ref[...]).astype(BF16)

    def piece(j):
        return jnp.dot(xn, wm_ref[:, j * HG_W:(j + 1) * HG_W], preferred_element_type=F32)

    for j in range(4):
        h4_ref[:, j * HG_W:(j + 1) * HG_W] = piece(j)
    aq_ref[...] = piece(4).astype(BF16)
    k = piece(5)
    k_ref[...] = k
    v = piece(6)
    v_ref[...] = v
    vb_ref[...] = v.astype(BF16)
    iq_ref[...] = piece(7).astype(BF16)

    s = jnp.dot(xn, ws_ref[...], preferred_element_type=F32)
    lane = lax.broadcasted_iota(jnp.int32, s.shape, 1)
    is_key = lane < IDX_DIM
    sk = jnp.where(is_key, s, 0.0)
    mu = jnp.sum(sk, axis=-1, keepdims=True) * (1.0 / IDX_DIM)
    cen = jnp.where(is_key, s - mu, 0.0)
    var = jnp.sum(cen * cen, axis=-1, keepdims=True) * (1.0 / IDX_DIM)
    ki = cen * lax.rsqrt(var + EPS) * kng_ref[...] + knb_ref[...]
    wi = (s * (IDX_HEADS ** -0.5)) * (IDX_DIM ** -0.5)
    small = jnp.where(is_key, ki, jnp.where(lane < _SMALL_W, wi, 0.0))
    small_ref[...] = small

    gate_ref[...] = _sigmoid(jnp.dot(xn, wg_ref[...], preferred_element_type=F32)).astype(BF16)

    if maybe_t_refs:
        kT_ref, kiT_ref = maybe_t_refs
        kT_ref[0] = k.T.astype(BF16)
        kiT = small.T[:IDX_DIM].astype(BF16)
        kiT_ref[0, :IDX_DIM, :] = kiT
        kiT_ref[0, IDX_DIM:, :] = kiT


def _inproj(x2, norm_g, wm, ws, wg, kng, knb, *, seq, tm, transposed):
    n = x2.shape[0]
    assert n % tm == 0 and (not transposed or seq % tm == 0)
    row = lambda w: pl.BlockSpec((tm, w), lambda i: (i, 0))
    out_shape = [
        jax.ShapeDtypeStruct((n, 4 * HG_W), F32),
        jax.ShapeDtypeStruct((n, ATT_W), BF16),
        jax.ShapeDtypeStruct((n, ATT_W), F32),
        jax.ShapeDtypeStruct((n, ATT_W), F32),
        jax.ShapeDtypeStruct((n, ATT_W), BF16),
        jax.ShapeDtypeStruct((n, IDX_W), BF16),
        jax.ShapeDtypeStruct((n, LANES), F32),
        jax.ShapeDtypeStruct((n, _GATE_W), BF16),
    ]
    out_specs = [row(4 * HG_W), row(ATT_W), row(ATT_W), row(ATT_W), row(ATT_W), row(IDX_W),
                 row(LANES), row(_GATE_W)]
    if transposed:
        per = seq // tm
        out_shape += [jax.ShapeDtypeStruct((n // seq, ATT_W, seq), BF16),
                      jax.ShapeDtypeStruct((n // seq, 2 * IDX_DIM, seq), BF16)]
        out_specs += [pl.BlockSpec((1, ATT_W, tm), lambda i: (i // per, 0, i % per)),
                      pl.BlockSpec((1, 2 * IDX_DIM, tm), lambda i: (i // per, 0, i % per))]
    return pl.pallas_call(
        _inproj_kernel,
        grid=(n // tm,),
        in_specs=[row(D_MODEL), _const_spec((1, D_MODEL)), _const_spec(wm.shape), _const_spec(ws.shape),
                  _const_spec(wg.shape), _const_spec((1, LANES)), _const_spec((1, LANES))],
        out_specs=out_specs,
        out_shape=out_shape,
        compiler_params=pltpu.CompilerParams(dimension_semantics=("arbitrary",),
                                             vmem_limit_bytes=VMEM_LIMIT),
        name="inproj",
    )(x2, norm_g, wm, ws, wg, kng, knb)


def _hgrn_tables(c):
    levels = int(math.log2(c))
    assert 2 ** levels == c
    t = np.arange(c)[:, None]
    u = np.arange(c)[None, :]
    sums = [(u <= t), (u > t)]
    masks = [np.eye(c, dtype=bool)]
    for lv in range(levels):
        m = 2 ** lv
        mid = (t // (2 * m)) * (2 * m) + m
        upper = (t % (2 * m)) >= m
        sums.append(upper & (u >= mid) & (u <= t))
        sums.append((~upper) & (u > t) & (u < mid))
        same = (t // (2 * m)) == (u // (2 * m))
        masks.append(same & upper & ((u % (2 * m)) < m))
    return (np.concatenate(sums, 0).astype(np.float32), np.stack(masks).astype(np.float32), levels)


def _hgrn_kernel(h4_ref, s0_ref, lbraw_ref, ong_ref, sums_ref, masks_ref, o_ref, sfin_ref, st_ref,
                 *, c, valid, levels, layer):
    ci = pl.program_id(1)

    @pl.when(ci == 0)
    def _():
        for h in range(HG_HEADS):
            st_ref[h] = s0_ref[0, h].T

    lbraw = lbraw_ref[...]
    e = jnp.exp(lbraw - jnp.max(lbraw, axis=0, keepdims=True))
    sm = e / jnp.sum(e, axis=0, keepdims=True)
    lb = jnp.sum(sm[:layer + 1], axis=0, keepdims=True)

    q = h4_ref[:, 0:HG_W]
    f = lb + (1.0 - lb) * _sigmoid(h4_ref[:, HG_W:2 * HG_W])
    logf = jnp.log(f)
    kk = 1.0 - f
    if valid < c:
        live = lax.broadcasted_iota(jnp.int32, f.shape, 0) < valid
        logf = jnp.where(live, logf, 0.0)
        kk = jnp.where(live, kk, 0.0)
    v = h4_ref[:, 2 * HG_W:3 * HG_W]

    p0 = logf.astype(BF16)
    r0 = logf - p0.astype(F32)
    p1 = r0.astype(BF16)
    p2 = (r0 - p1.astype(F32)).astype(BF16)
    sums = sums_ref[...]
    ex = (jnp.dot(sums, p0, preferred_element_type=F32) + jnp.dot(sums, p1, preferred_element_type=F32)
          + jnp.dot(sums, p2, preferred_element_type=F32))
    nt = (((1,), (1,)), ((), ()))
    tn = (((0,), (0,)), ((), ()))
    for h in range(HG_HEADS):
        hs = slice(h * HG_DK, (h + 1) * HG_DK)
        qh, kh, vh = q[:, hs], kk[:, hs], v[:, hs].astype(BF16)
        a = masks_ref[0] * lax.dot_general(qh.astype(BF16), kh.astype(BF16), nt, preferred_element_type=F32)
        for lv in range(levels):
            eq = ex[(2 + 2 * lv) * c:(3 + 2 * lv) * c, hs]
            ek = ex[(3 + 2 * lv) * c:(4 + 2 * lv) * c, hs]
            ql = (qh * jnp.exp(eq)).astype(BF16)
            kl = (kh * jnp.exp(ek)).astype(BF16)
            a = a + masks_ref[lv + 1] * lax.dot_general(ql, kl, nt, preferred_element_type=F32)
        st = st_ref[h]
        qi = (qh * jnp.exp(ex[0:c, hs])).astype(BF16)
        o = (jnp.dot(a.astype(BF16), vh, preferred_element_type=F32)
             + lax.dot_general(qi, st.astype(BF16), nt, preferred_element_type=F32))
        ke = (kh * jnp.exp(ex[c:2 * c, hs])).astype(BF16)
        st_ref[h] = st * jnp.exp(ex[c - 1:c, hs]) + lax.dot_general(vh, ke, tn, preferred_element_type=F32)
        on = _rms(o, ong_ref[...])
        g = h4_ref[:, 3 * HG_W + h * HG_DV:3 * HG_W + (h + 1) * HG_DV]
        o_ref[:, hs] = (on * (g * _sigmoid(g))).astype(BF16)

    @pl.when(ci == pl.num_programs(1) - 1)
    def _():
        for h in range(HG_HEADS):
            sfin_ref[0, h] = st_ref[h].T


def _hgrn(h4, s0, lbraw, ong, *, seq, chunk, valid, layer):
    n = h4.shape[0]
    b = n // seq
    c = chunk
    nc = seq // c
    assert seq % c == 0 and (valid == c or nc == 1)
    sums, masks, levels = _hgrn_tables(c)
    sums = jnp.asarray(sums, BF16)
    masks = jnp.asarray(masks, F32)
    return pl.pallas_call(
        functools.partial(_hgrn_kernel, c=c, valid=valid, levels=levels, layer=layer),
        grid=(b, nc),
        in_specs=[pl.BlockSpec((c, 4 * HG_W), lambda i, j: (i * nc + j, 0)),
                  pl.BlockSpec((1, HG_HEADS, HG_DK, HG_DV), lambda i, j: (i, 0, 0, 0)),
                  _const_spec(lbraw.shape), _const_spec((1, HG_DV)), _const_spec(sums.shape),
                  _const_spec(masks.shape)],
        out_specs=[pl.BlockSpec((c, HG_W), lambda i, j: (i * nc + j, 0)),
                   pl.BlockSpec((1, HG_HEADS, HG_DK, HG_DV), lambda i, j: (i, 0, 0, 0))],
        out_shape=[jax.ShapeDtypeStruct((n, HG_W), BF16),
                   jax.ShapeDtypeStruct((b, HG_HEADS, HG_DK, HG_DV), F32)],
        scratch_shapes=[pltpu.VMEM((HG_HEADS, HG_DV, HG_DK), F32)],
        compiler_params=pltpu.CompilerParams(dimension_semantics=("arbitrary", "arbitrary"),
                                             vmem_limit_bytes=VMEM_LIMIT),
        name="hgrn",
    )(h4, s0, lbraw, ong, sums, masks)


_KEY_NEG_INF = int(np.array([-np.inf], np.float32).view(np.int32)[0]) ^ 0x7FFFFFFF
_INT_MAX = 2 ** 31 - 1


def _sort_key(s):
    s = jnp.where(s == 0.0, 0.0, s)
    bits = pltpu.bitcast(s, jnp.int32)
    return jnp.where(bits < 0, bits ^ 0x7FFFFFFF, bits)


def _fold_lanes(x):
    acc = x[:, :LANES]
    for j in range(1, x.shape[1] // LANES):
        acc = acc + x[:, j * LANES:(j + 1) * LANES]
    return acc


def _tile_lanes(x, n):
    return x if n == 1 else jnp.concatenate([x] * n, axis=1)


def _topk_threshold(key_ref, t_ref, n_ref, cand_ref, acc_ref, *, nrows, rg, ntiles, tk, topk, idx_bits):
    ones = jnp.ones((LANES, LANES), BF16)
    lane = lax.broadcasted_iota(jnp.int32, (rg, LANES), 1)
    groups = [slice(r * rg, (r + 1) * rg) for r in range(nrows // rg)]

    def for_tiles(rows, body, init):
        def step(kt, carry):
            for j in range(tk // LANES):
                c0 = pl.multiple_of(kt * tk + j * LANES, LANES)
                carry = body(rows, c0, carry)
            return carry
        return lax.fori_loop(0, ntiles, step, init)

    def count(indicator):
        for rows in groups:
            acc_ref[rows, :] = for_tiles(
                rows, lambda rows, c0, acc: acc + indicator(key_ref[rows, pl.ds(c0, LANES)], c0, rows),
                jnp.zeros((rg, LANES), F32))
        return jnp.dot(acc_ref[...].astype(BF16), ones, preferred_element_type=F32)

    t_ref[...] = jnp.full((nrows, LANES), INT_MIN, jnp.int32)
    n_ref[...] = jnp.full((nrows, LANES), ntiles * tk, jnp.int32)

    def bit_step(i, carry):
        cand_ref[...] = t_ref[...] + jnp.left_shift(jnp.int32(1), 31 - i)
        c = count(lambda kk, c0, rows: jnp.where(kk >= cand_ref[rows, :], 1.0, 0.0))
        ok = c >= topk
        t_ref[...] = jnp.where(ok, cand_ref[...], t_ref[...])
        n_ref[...] = jnp.where(ok, c.astype(jnp.int32), n_ref[...])
        return carry
    lax.fori_loop(0, 32, bit_step, 0)

    t = t_ref[...]
    is_inf = t == _KEY_NEG_INF
    has_cut = jnp.where(n_ref[...] > topk, jnp.where(is_inf, 0, 1), 0)

    @pl.when(jnp.max(has_cut) > 0)
    def _():
        n_gt = count(lambda kk, c0, rows: jnp.where(kk > t_ref[rows, :], 1.0, 0.0))
        n_ref[...] = topk - n_gt.astype(jnp.int32)
        cand_ref[...] = jnp.zeros((nrows, LANES), jnp.int32)

        def idx_step(i, carry):
            step = jnp.left_shift(jnp.int32(1), idx_bits - 1 - i)
            below = count(lambda kk, c0, rows: jnp.where(
                kk == t_ref[rows, :], jnp.where(c0 + lane < cand_ref[rows, :] + step, 1.0, 0.0), 0.0))
            cand_ref[...] = jnp.where(below.astype(jnp.int32) < n_ref[...], cand_ref[...] + step, cand_ref[...])
            return carry
        lax.fori_loop(0, idx_bits, idx_step, 0)
        cand_ref[...] = jnp.where(has_cut > 0, cand_ref[...], _INT_MAX)

        def drop(rows, c0, carry):
            kk = key_ref[rows, pl.ds(c0, LANES)]
            late = jnp.where(kk == t_ref[rows, :], jnp.where(c0 + lane > cand_ref[rows, :], 1, 0), 0)
            key_ref[rows, pl.ds(c0, LANES)] = jnp.where(late > 0, INT_MIN, kk)
            return carry
        for rows in groups:
            for_tiles(rows, drop, 0)

    t_ref[...] = jnp.where(is_inf, t + 1, t)


def _pattn_kernel(q_ref, iq_ref, small_ref, kT_ref, vb_ref, kiT_ref, o_ref,
                  qm_scr, iqm_scr, key_scr, t_scr, n_scr, cand_scr, cnt_scr, m_scr, alpha_scr, acc_scr,
                  lga_scr, lgb_scr, p_scr,
                  *, tq, tk, rg, rgs, topk, idx_bits):
    qb = pl.program_id(1)
    q0 = qb * tq
    ntiles = (q0 + tq) // tk
    lane = lax.broadcasted_iota(jnp.int32, (tq, LANES), 1)
    for h in range(ATT_HEADS):
        ps = slice((h // 2) * LANES, (h // 2 + 1) * LANES)
        mine = (lane < ATT_HD) if h % 2 == 0 else (lane >= ATT_HD)
        qm_scr[h] = jnp.where(mine, q_ref[:, ps] * (ATT_HD ** -0.5), 0.0).astype(BF16)
        iqm_scr[h] = jnp.where(mine, iq_ref[:, ps], 0.0).astype(BF16)
    qpos = q0 + lax.broadcasted_iota(jnp.int32, (tq, 1), 0)

    def score_tile(kt, carry):
        k0 = pl.multiple_of(kt * tk, tk)
        kit = kiT_ref[0, :, pl.ds(k0, tk)]
        acc = jnp.zeros((tq, tk), F32)
        for h in range(IDX_HEADS):
            d = jnp.dot(iqm_scr[h], kit, preferred_element_type=F32)
            acc = acc + small_ref[:, IDX_DIM + h:IDX_DIM + h + 1] * jnp.maximum(d, 0.0)
        kpos = k0 + lax.broadcasted_iota(jnp.int32, (tq, tk), 1)
        key_scr[:, pl.ds(k0, tk)] = _sort_key(jnp.where(kpos <= qpos, acc, -jnp.inf))
        return carry
    lax.fori_loop(0, ntiles, score_tile, 0)

    _topk_threshold(key_scr, t_scr, n_scr, cand_scr, cnt_scr, nrows=tq, rg=rgs, ntiles=ntiles, tk=tk,
                    topk=topk, idx_bits=idx_bits)

    m_scr[...] = jnp.full(m_scr.shape, NEG_BIG, F32)
    acc_scr[...] = jnp.zeros(acc_scr.shape, F32)
    nl = tk // LANES
    vlane = lax.broadcasted_iota(jnp.int32, (tk, LANES), 1)

    def logits(kt, lg_ref):
        k0 = pl.multiple_of(jnp.minimum(kt, ntiles - 1) * tk, tk)
        for h in range(ATT_HEADS):
            ps = slice((h // 2) * LANES, (h // 2 + 1) * LANES)
            lg_ref[h] = jnp.dot(qm_scr[h], kT_ref[0, ps, pl.ds(k0, tk)], preferred_element_type=F32)

    def attend(kt, lg_ref, half):
        live = kt < ntiles
        k0 = pl.multiple_of(jnp.minimum(kt, ntiles - 1) * tk, tk)
        for r in range(tq // rg):
            rows = slice(r * rg, (r + 1) * rg)
            thr = jnp.where(live, t_scr[rows, :], _INT_MAX)
            bias = jnp.where(key_scr[rows, pl.ds(k0, tk)] >= _tile_lanes(thr, nl), 0.0, NEG_BIG)
            for h in range(ATT_HEADS):
                lg = lg_ref[h, rows, :] + bias
                m_old = m_scr[h, rows, :]
                m_new = jnp.maximum(m_old, jnp.max(lg, axis=-1, keepdims=True))
                alpha_scr[half, h, rows, :] = jnp.exp(m_old - m_new)
                p_scr[half, h, rows, :] = jnp.exp(lg - _tile_lanes(m_new, nl)).astype(BF16)
                m_scr[h, rows, :] = m_new
        for h in range(ATT_HEADS):
            ps = slice((h // 2) * LANES, (h // 2 + 1) * LANES)
            mine = (vlane < ATT_HD) if h % 2 == 0 else (vlane >= ATT_HD)
            w = jnp.where(mine, vb_ref[pl.ds(k0, tk), ps], 1.0)
            acc_scr[h] = (alpha_scr[half, h] * acc_scr[h]
                          + jnp.dot(p_scr[half, h], w, preferred_element_type=F32))

    def attend_pair(i, carry):
        logits(2 * i + 1, lgb_scr)
        attend(2 * i, lga_scr, 0)
        logits(2 * i + 2, lga_scr)
        attend(2 * i + 1, lgb_scr, 1)
        return carry
    logits(0, lga_scr)
    lax.fori_loop(0, (ntiles + 1) // 2, attend_pair, 0)

    for hp in range(ATT_HEADS // 2):
        even, odd = acc_scr[2 * hp], acc_scr[2 * hp + 1]
        even = even / pltpu.roll(even, ATT_HD, 1)
        odd = odd / pltpu.roll(odd, ATT_HD, 1)
        o_ref[:, hp * LANES:(hp + 1) * LANES] = jnp.where(lane < ATT_HD, even, odd).astype(BF16)


def _pattn(aq, iq, small, kT, vb, kiT, *, seq, tq, tk, rg):
    n = aq.shape[0]
    b = n // seq
    topk = min(TOPK_MAX, seq // TOPK_FRAC)
    assert seq % tq == 0 and tq % tk == 0 and tk >= topk and tk % LANES == 0 and tq % rg == 0
    assert seq // LANES < 256
    nq = seq // tq
    row = lambda w: pl.BlockSpec((tq, w), lambda i, j: (i * nq + j, 0))
    return pl.pallas_call(
        functools.partial(_pattn_kernel, tq=tq, tk=tk, rg=rg, rgs=min(tq, 128), topk=topk,
                          idx_bits=(seq - 1).bit_length()),
        grid=(b, nq),
        in_specs=[row(ATT_W), row(IDX_W), row(LANES),
                  pl.BlockSpec((1, ATT_W, seq), lambda i, j: (i, 0, 0)),
                  pl.BlockSpec((seq, ATT_W), lambda i, j: (i, 0)),
                  pl.BlockSpec((1, 2 * IDX_DIM, seq), lambda i, j: (i, 0, 0))],
        out_specs=row(ATT_W),
        out_shape=jax.ShapeDtypeStruct((n, ATT_W), BF16),
        scratch_shapes=[pltpu.VMEM((ATT_HEADS, tq, LANES), BF16), pltpu.VMEM((IDX_HEADS, tq, LANES), BF16),
                        pltpu.VMEM((tq, seq), jnp.int32),
                        pltpu.VMEM((tq, LANES), jnp.int32), pltpu.VMEM((tq, LANES), jnp.int32),
                        pltpu.VMEM((tq, LANES), jnp.int32), pltpu.VMEM((tq, LANES), F32),
                        pltpu.VMEM((ATT_HEADS, tq, LANES), F32), pltpu.VMEM((2, ATT_HEADS, tq, LANES), F32),
                        pltpu.VMEM((ATT_HEADS, tq, LANES), F32),
                        pltpu.VMEM((ATT_HEADS, tq, tk), F32), pltpu.VMEM((ATT_HEADS, tq, tk), F32),
                        pltpu.VMEM((2, ATT_HEADS, tq, tk), BF16)],
        compiler_params=pltpu.CompilerParams(dimension_semantics=("arbitrary", "arbitrary"),
                                             vmem_limit_bytes=VMEM_LIMIT),
        name="prompt_attn",
    )(aq, iq, small, kT, vb, kiT)


def _sscore_kernel(pt_ref, iq_ref, w_ref, kinew_ref, *rest, pp, tnew):
    page_refs, (keys_ref, keysnew_ref) = rest[:pp], rest[pp:]
    j = pl.program_id(1)
    nt = (((1,), (1,)), ((), ()))
    iq = iq_ref[0]
    w = w_ref[0]

    def scores(keys_bf16):
        d = lax.dot_general(iq, keys_bf16, nt, preferred_element_type=F32)
        r = w * jnp.maximum(d, 0.0)
        acc = jnp.zeros((tnew, LANES), F32)
        for h in range(IDX_HEADS):
            acc = acc + r[h * tnew:(h + 1) * tnew]
        return acc

    for i in range(pp):
        keys_ref[:, i * PAGE_SIZE:(i + 1) * PAGE_SIZE] = _sort_key(scores(page_refs[i][0].astype(BF16)))

    @pl.when(j == pl.num_programs(1) - 1)
    def _():
        s = scores(kinew_ref[0])
        qrow = lax.broadcasted_iota(jnp.int32, (tnew, LANES), 0)
        kcol = lax.broadcasted_iota(jnp.int32, (tnew, LANES), 1)
        keysnew_ref[...] = _sort_key(jnp.where(kcol <= qrow, s, -jnp.inf))


def _sscore(page_table, iq_s, w_s, kinew, cache_kidx, *, pp):
    b, n_pages = page_table.shape
    tnew = iq_s.shape[1] // IDX_HEADS
    past = n_pages * PAGE_SIZE
    assert n_pages % pp == 0 and tnew <= LANES and tnew % 8 == 0
    page_specs = [pl.BlockSpec((1, PAGE_SIZE, IDX_DIM), lambda i, j, pt, s=s: (pt[i, j * pp + s], 0, 0))
                  for s in range(pp)]
    grid_spec = pltpu.PrefetchScalarGridSpec(
        num_scalar_prefetch=1,
        grid=(b, n_pages // pp),
        in_specs=[pl.BlockSpec((1, IDX_HEADS * tnew, IDX_DIM), lambda i, j, pt: (i, 0, 0)),
                  pl.BlockSpec((1, IDX_HEADS * tnew, 1), lambda i, j, pt: (i, 0, 0)),
                  pl.BlockSpec((1, LANES, IDX_DIM), lambda i, j, pt: (i, 0, 0))] + page_specs,
        out_specs=[pl.BlockSpec((tnew, pp * PAGE_SIZE), lambda i, j, pt: (i, j)),
                   pl.BlockSpec((tnew, LANES), lambda i, j, pt: (i, 0))])
    return pl.pallas_call(
        functools.partial(_sscore_kernel, pp=pp, tnew=tnew),
        grid_spec=grid_spec,
        out_shape=[jax.ShapeDtypeStruct((b * tnew, past), jnp.int32),
                   jax.ShapeDtypeStruct((b * tnew, LANES), jnp.int32)],
        compiler_params=pltpu.CompilerParams(dimension_semantics=("arbitrary", "arbitrary"),
                                             vmem_limit_bytes=VMEM_LIMIT),
        name="sample_scores",
    )(page_table, iq_s, w_s, kinew, *([cache_kidx] * pp))


def _stopk_kernel(keys_ref, keysnew_ref, bias_ref, key_scr, t_scr, n_scr, cand_scr, cnt_scr,
                  *, rows, past, ntiles, tk, topk, idx_bits):
    key_scr[:, :past] = keys_ref[...]
    key_scr[:, past:] = keysnew_ref[...]
    _topk_threshold(key_scr, t_scr, n_scr, cand_scr, cnt_scr, nrows=rows, rg=rows, ntiles=ntiles, tk=tk,
                    topk=topk, idx_bits=idx_bits)
    t = t_scr[...]
    for c0 in range(0, past + LANES, LANES):
        bias_ref[:, c0:c0 + LANES] = jnp.where(key_scr[:, c0:c0 + LANES] >= t, 0.0, NEG_BIG)


def _stopk(keys, keysnew, *, tnew, group):
    n, past = keys.shape
    total = past + LANES
    rows = group * tnew
    topk = min(TOPK_MAX, (past + tnew) // TOPK_FRAC)
    nl = total // LANES
    per = max(d for d in range(1, 17) if nl % d == 0)
    assert n % rows == 0 and nl < 256
    return pl.pallas_call(
        functools.partial(_stopk_kernel, rows=rows, past=past, ntiles=nl // per, tk=per * LANES, topk=topk,
                          idx_bits=(total - 1).bit_length()),
        grid=(n // rows,),
        in_specs=[pl.BlockSpec((rows, past), lambda i: (i, 0)), pl.BlockSpec((rows, LANES), lambda i: (i, 0))],
        out_specs=pl.BlockSpec((rows, total), lambda i: (i, 0)),
        out_shape=jax.ShapeDtypeStruct((n, total), F32),
        scratch_shapes=[pltpu.VMEM((rows, total), jnp.int32), pltpu.VMEM((rows, LANES), jnp.int32),
                        pltpu.VMEM((rows, LANES), jnp.int32), pltpu.VMEM((rows, LANES), jnp.int32),
                        pltpu.VMEM((rows, LANES), F32)],
        compiler_params=pltpu.CompilerParams(dimension_semantics=("arbitrary",), vmem_limit_bytes=VMEM_LIMIT),
        name="sample_topk",
    )(keys, keysnew)


def _sattn_kernel(pt_ref, q_ref, bias_ref, tail_ref, knew_ref, vnew_ref, *rest, pp, tnew):
    k_refs, v_refs = rest[:pp], rest[pp:2 * pp]
    o_ref, m_scr, l_scr, acc_scr = rest[2 * pp:]
    j = pl.program_id(1)
    nt = (((1,), (1,)), ((), ()))

    @pl.when(j == 0)
    def _():
        m_scr[...] = jnp.full(m_scr.shape, NEG_BIG, F32)
        l_scr[...] = jnp.zeros(l_scr.shape, F32)
        acc_scr[...] = jnp.zeros(acc_scr.shape, F32)

    def update(keys_of, values_of, bias):
        lgs = [lax.dot_general(q_ref[0, h], keys_of(h), nt, preferred_element_type=F32) + bias
               for h in range(ATT_HEADS)]
        ps = []
        for h, lg in enumerate(lgs):
            m_old = m_scr[h]
            m_new = jnp.maximum(m_old, jnp.max(lg, axis=-1, keepdims=True))
            alpha = jnp.exp(m_old - m_new)
            p = jnp.exp(lg - _tile_lanes(m_new, lg.shape[1] // LANES))
            l_scr[h] = alpha * l_scr[h] + jnp.sum(p, axis=-1, keepdims=True)
            m_scr[h] = m_new
            ps.append((alpha[:, :ATT_HD], p.astype(BF16)))
        for h, (alpha, p) in enumerate(ps):
            acc_scr[h] = alpha * acc_scr[h] + jnp.dot(p, values_of(h), preferred_element_type=F32)

    def head_rows(refs, h):
        return jnp.concatenate([r[pl.ds(h, PAGE_SIZE, stride=ATT_HEADS), :] for r in refs], axis=0).astype(BF16)

    update(lambda h: head_rows(k_refs, h), lambda h: head_rows(v_refs, h), bias_ref[...])

    @pl.when(j == pl.num_programs(1) - 1)
    def _():
        update(lambda h: knew_ref[0, h], lambda h: vnew_ref[0, h], tail_ref[...])
        o_ref[0] = jnp.concatenate([acc_scr[h] / l_scr[h][:, :ATT_HD] for h in range(ATT_HEADS)],
                                   axis=-1).astype(BF16)


def _sattn(page_table, q_hm, bias, knew_hm, vnew_hm, cache_k, cache_v, *, layer, pp):
    b, n_pages = page_table.shape
    tnew = q_hm.shape[2]
    kv_specs = [pl.BlockSpec((None, None, PAGE_SIZE * ATT_HEADS, ATT_HD),
                             lambda i, j, pt, s=s: (layer, pt[i, j * pp + s], 0, 0)) for s in range(pp)]
    grid_spec = pltpu.PrefetchScalarGridSpec(
        num_scalar_prefetch=1,
        grid=(b, n_pages // pp),
        in_specs=[pl.BlockSpec((1, ATT_HEADS, tnew, ATT_HD), lambda i, j, pt: (i, 0, 0, 0)),
                  pl.BlockSpec((tnew, pp * PAGE_SIZE), lambda i, j, pt: (i, j)),
                  pl.BlockSpec((tnew, LANES), lambda i, j, pt: (i, n_pages)),
                  pl.BlockSpec((1, ATT_HEADS, LANES, ATT_HD), lambda i, j, pt: (i, 0, 0, 0)),
                  pl.BlockSpec((1, ATT_HEADS, LANES, ATT_HD), lambda i, j, pt: (i, 0, 0, 0))] + kv_specs + kv_specs,
        out_specs=pl.BlockSpec((1, tnew, ATT_W), lambda i, j, pt: (i, 0, 0)),
        scratch_shapes=[pltpu.VMEM((ATT_HEADS, tnew, LANES), F32), pltpu.VMEM((ATT_HEADS, tnew, LANES), F32),
                        pltpu.VMEM((ATT_HEADS, tnew, ATT_HD), F32)])
    return pl.pallas_call(
        functools.partial(_sattn_kernel, pp=pp, tnew=tnew),
        grid_spec=grid_spec,
        out_shape=jax.ShapeDtypeStruct((b, tnew, ATT_W), BF16),
        compiler_params=pltpu.CompilerParams(dimension_semantics=("arbitrary", "arbitrary"),
                                             vmem_limit_bytes=VMEM_LIMIT),
        name="sample_attn",
    )(page_table, q_hm, bias, bias, knew_hm, vnew_hm, *([cache_k] * pp), *([cache_v] * pp))


def _mixffn_kernel(x_ref, oa_ref, ob_ref, gate_ref, wb_ref, wo_ref, g2_ref, wfi_ref, wfo_ref, gf_ref, y_ref,
                   *, d_ff, ff_chunk, final):
    ga = gate_ref[:, :D_MODEL].astype(F32)
    gb = gate_ref[:, D_MODEL:].astype(F32)
    merged = (jnp.dot(oa_ref[...], wb_ref[0], preferred_element_type=F32) * ga
              + jnp.dot(ob_ref[...], wb_ref[1], preferred_element_type=F32) * gb)
    x1 = x_ref[...] + jnp.dot(merged.astype(BF16), wo_ref[...], preferred_element_type=F32)
    xn = _rms(x1, g2_ref[...]).astype(BF16)
    acc = x1
    for c0 in range(0, d_ff, ff_chunk):
        a = jnp.dot(xn, wfi_ref[:, c0:c0 + ff_chunk], preferred_element_type=F32)
        bgate = jnp.dot(xn, wfi_ref[:, d_ff + c0:d_ff + c0 + ff_chunk], preferred_element_type=F32)
        hact = (a * _sigmoid(a) * bgate).astype(BF16)
        acc = acc + jnp.dot(hact, wfo_ref[c0:c0 + ff_chunk, :], preferred_element_type=F32)
    y_ref[...] = _rms(acc, gf_ref[...]) if final else acc


def _mixffn(x2, oa, ob, gate, wb, wo, g2, wfi, wfo, gf, *, tm, final):
    n = x2.shape[0]
    d_ff = wfo.shape[0]
    ff_chunk = 2 * LANES if d_ff % (2 * LANES) == 0 else d_ff
    row = lambda w: pl.BlockSpec((tm, w), lambda i: (i, 0))
    return pl.pallas_call(
        functools.partial(_mixffn_kernel, d_ff=d_ff, ff_chunk=ff_chunk, final=final),
        grid=(n // tm,),
        in_specs=[row(D_MODEL), row(HG_W), row(ATT_W), row(_GATE_W), _const_spec(wb.shape), _const_spec(wo.shape),
                  _const_spec((1, D_MODEL)), _const_spec(wfi.shape), _const_spec(wfo.shape),
                  _const_spec((1, D_MODEL))],
        out_specs=row(D_MODEL),
        out_shape=jax.ShapeDtypeStruct((n, D_MODEL), F32),
        compiler_params=pltpu.CompilerParams(dimension_semantics=("arbitrary",),
                                             vmem_limit_bytes=VMEM_LIMIT),
        name="mix_ffn",
    )(x2, oa, ob, gate, wb, wo, g2, wfi, wfo, gf)


def _pick_tile(n, pref):
    t = min(n, pref)
    while n % t:
        t //= 2
    return t


def kernel(x_prompt, x_sample, cache_k, cache_v, cache_kidx, state_hgrn, page_table, norm1_g, w_in, hg_lb_raw,
           hg_onorm_g, idx_knorm_g, idx_knorm_b, w_branch, w_out, norm2_g, w_ffn_in, w_ffn_out, final_g):
    bp, tp, _ = x_prompt.shape
    bs, ts, _ = x_sample.shape
    depth = w_in.shape[0]
    n_pool = cache_k.shape[1]
    xp = x_prompt.reshape(bp * tp, D_MODEL)
    xs = x_sample.reshape(bs * ts, D_MODEL)
    row = lambda a: a.reshape(1, -1)
    pad_lanes = lambda a: jnp.pad(a, (0, LANES - a.shape[0])).reshape(1, LANES)
    outs = [[] for _ in range(8)]
    for l in range(depth):
        wm = w_in[l, :, :_MAIN_W].astype(BF16)
        ws = jnp.pad(w_in[l, :, _MAIN_W:_MAIN_W + _SMALL_W], ((0, 0), (0, LANES - _SMALL_W))).astype(BF16)
        wg = w_in[l, :, _MAIN_W + _SMALL_W:].astype(BF16)
        kng, knb = pad_lanes(idx_knorm_g[l]), pad_lanes(idx_knorm_b[l])
        proj = functools.partial(_inproj, norm_g=row(norm1_g[l]), wm=wm, ws=ws, wg=wg, kng=kng, knb=knb)
        mix = functools.partial(_mixffn, wb=w_branch[l].astype(BF16), wo=w_out[l].astype(BF16), g2=row(norm2_g[l]),
                                wfi=w_ffn_in[l].astype(BF16), wfo=w_ffn_out[l].astype(BF16), gf=row(final_g),
                                final=(l == depth - 1))
        hgrn = functools.partial(_hgrn, lbraw=hg_lb_raw, ong=row(hg_onorm_g[l]), layer=l)

        tm = _pick_tile(tp, 512)
        h4, aq, k, v, vb, iq, small, gate, kT, kiT = proj(xp, seq=tp, tm=tm, transposed=True)
        cp = math.gcd(tp, HG_CHUNK)
        oa, s_p = hgrn(h4, jnp.zeros((bp, HG_HEADS, HG_DK, HG_DV), F32), seq=tp, chunk=cp, valid=cp)
        tq = _pick_tile(tp, 256)
        ob = _pattn(aq, iq, small, kT, vb, kiT, seq=tp, tq=tq, tk=tq, rg=min(tq, 32))
        xp = mix(xp, oa, ob, gate, tm=tm)
        outs[0].append(k.reshape(bp, tp, ATT_HEADS, ATT_HD))
        outs[1].append(v.reshape(bp, tp, ATT_HEADS, ATT_HD))
        outs[2].append(small[:, :IDX_DIM].reshape(bp, tp, IDX_DIM))
        outs[3].append(s_p)

        ns = bs * ts
        tms = _pick_tile(ns, 512)
        h4, aq, k, v, vb, iq, small, gate = proj(xs, seq=ts, tm=tms, transposed=False)
        cs = 16
        assert ts <= cs
        h4p = jnp.pad(h4.reshape(bs, ts, 4 * HG_W), ((0, 0), (0, cs - ts), (0, 0))).reshape(bs * cs, 4 * HG_W)
        oa, s_s = hgrn(h4p, state_hgrn[l], seq=cs, chunk=cs, valid=ts)
        oa = oa.reshape(bs, cs, HG_W)[:, :ts].reshape(ns, HG_W)
        iq_s = iq.reshape(bs, ts, IDX_HEADS, IDX_DIM).transpose(0, 2, 1, 3).reshape(bs, IDX_HEADS * ts, IDX_DIM)
        w_s = small[:, IDX_DIM:_SMALL_W].reshape(bs, ts, IDX_HEADS).transpose(0, 2, 1).reshape(bs, IDX_HEADS * ts, 1)
        pad_rows = lambda a: jnp.pad(a.reshape(bs, ts, -1), ((0, 0), (0, LANES - ts), (0, 0))).astype(BF16)
        n_pages = page_table.shape[1]
        keys, keysnew = _sscore(page_table, iq_s, w_s, pad_rows(small[:, :IDX_DIM]),
                                cache_kidx[l].reshape(n_pool, PAGE_SIZE, IDX_DIM), pp=_pick_tile(n_pages, 16))
        bias = _stopk(keys, keysnew, tnew=ts, group=_pick_tile(bs, 8))
        head_major = lambda a: a.reshape(bs, -1, ATT_HEADS, ATT_HD).transpose(0, 2, 1, 3)
        rows_heads = lambda c: c.reshape(c.shape[0], n_pool, PAGE_SIZE * ATT_HEADS, ATT_HD)
        ob = _sattn(page_table, head_major(aq * (ATT_HD ** -0.5)), bias, head_major(pad_rows(k)),
                    head_major(pad_rows(v)), rows_heads(cache_k), rows_heads(cache_v), layer=l,
                    pp=_pick_tile(n_pages, 8)).reshape(ns, ATT_W)
        xs = mix(xs, oa, ob, gate, tm=tms)
        outs[4].append(k.reshape(bs, ts, ATT_HEADS, ATT_HD))
        outs[5].append(v.reshape(bs, ts, ATT_HEADS, ATT_HD))
        outs[6].append(small[:, :IDX_DIM].reshape(bs, ts, IDX_DIM))
        outs[7].append(s_s)
    return (xp.reshape(bp, tp, D_MODEL), xs.reshape(bs, ts, D_MODEL)) + tuple(jnp.stack(o) for o in outs)
```

```python
import functools
import math

import numpy as np
import jax
import jax.numpy as jnp
from jax import lax
from jax.experimental import pallas as pl
from jax.experimental.pallas import tpu as pltpu

F32 = jnp.float32
BF16 = jnp.bfloat16

D_MODEL = 1024
HG_HEADS, HG_DK, HG_DV = 4, 128, 128
HG_W = HG_HEADS * HG_DK
HG_CHUNK = 128
ATT_HEADS, ATT_HD = 8, 64
ATT_W = ATT_HEADS * ATT_HD
IDX_HEADS, IDX_DIM = 8, 64
IDX_W = IDX_HEADS * IDX_DIM
TOPK_MAX, TOPK_FRAC = 256, 4
PAGE_SIZE = 128
N_BRANCH = 2
EPS = 1e-6
LANES = 128
VMEM_LIMIT = 56 * 1024 * 1024

_MAIN_W = 4 * HG_W + 3 * ATT_W + IDX_W
_SMALL_W = IDX_DIM + IDX_HEADS
_GATE_W = N_BRANCH * D_MODEL

NEG_BIG = -1e30
INT_MIN = -(2 ** 31)


def _const_spec(shape):
    nd = len(shape)
    return pl.BlockSpec(shape, lambda *_: (0,) * nd)


def _rms(xf, g):
    return xf * lax.rsqrt(jnp.mean(xf * xf, axis=-1, keepdims=True) + EPS) * g


def _sigmoid(x):
    return 1.0 / (1.0 + jnp.exp(-x))


def _inproj_kernel(x_ref, g_ref, wm_ref, ws_ref, wg_ref, kng_ref, knb_ref,
                   h4_ref, aq_ref, vb_ref, iq_ref, small_ref, gate_ref, *kv_refs, transposed):
    xn = _rms(x_ref[...], g_ref[...]).astype(BF16)

    def piece(j):
        return jnp.dot(xn, wm_ref[:, j * HG_W:(j + 1) * HG_W], preferred_element_type=F32)

    for j in range(4):
        h4_ref[:, j * HG_W:(j + 1) * HG_W] = piece(j)
    aq_ref[...] = piece(4).astype(BF16)
    k = piece(5)
    v = piece(6)
    vb_ref[...] = v.astype(BF16)
    iq_ref[...] = piece(7).astype(BF16)

    s = jnp.dot(xn, ws_ref[...], preferred_element_type=F32)
    lane = lax.broadcasted_iota(jnp.int32, s.shape, 1)
    is_key = lane < IDX_DIM
    sk = jnp.where(is_key, s, 0.0)
    mu = jnp.sum(sk, axis=-1, keepdims=True) * (1.0 / IDX_DIM)
    cen = jnp.where(is_key, s - mu, 0.0)
    var = jnp.sum(cen * cen, axis=-1, keepdims=True) * (1.0 / IDX_DIM)
    ki = cen * lax.rsqrt(var + EPS) * kng_ref[...] + knb_ref[...]
    wi = (s * (IDX_HEADS ** -0.5)) * (IDX_DIM ** -0.5)
    small = jnp.where(is_key, ki, jnp.where(lane < _SMALL_W, wi, 0.0))
    small_ref[...] = small

    gate_ref[...] = _sigmoid(jnp.dot(xn, wg_ref[...], preferred_element_type=F32)).astype(BF16)

    if transposed:
        kt_ref, vt_ref, kit_ref, ktb_ref, kitb_ref = kv_refs
        kt = k.T
        kt_ref[0] = kt
        vt_ref[0] = v.T
        ktb_ref[0] = kt.astype(BF16)
        kit = small.T[:IDX_DIM]
        kit_ref[0] = kit
        kitb_ref[0, :IDX_DIM, :] = kit.astype(BF16)
        kitb_ref[0, IDX_DIM:, :] = kit.astype(BF16)
    else:
        k_ref, v_ref = kv_refs
        k_ref[...] = k
        v_ref[...] = v


def _inproj(x2, norm_g, wm, ws, wg, kng, knb, *, seq, tm, transposed):
    n = x2.shape[0]
    assert n % tm == 0 and (not transposed or seq % tm == 0)
    row = lambda w: pl.BlockSpec((tm, w), lambda i: (i, 0))
    out_shape = [
        jax.ShapeDtypeStruct((n, 4 * HG_W), F32),
        jax.ShapeDtypeStruct((n, ATT_W), BF16),
        jax.ShapeDtypeStruct((n, ATT_W), BF16),
        jax.ShapeDtypeStruct((n, IDX_W), BF16),
        jax.ShapeDtypeStruct((n, LANES), F32),
        jax.ShapeDtypeStruct((n, _GATE_W), BF16),
    ]
    out_specs = [row(4 * HG_W), row(ATT_W), row(ATT_W), row(IDX_W), row(LANES), row(_GATE_W)]
    if transposed:
        per, b = seq // tm, n // seq
        col = lambda w: pl.BlockSpec((1, w, tm), lambda i: (i // per, 0, i % per))
        out_shape += [jax.ShapeDtypeStruct((b, ATT_W, seq), F32), jax.ShapeDtypeStruct((b, ATT_W, seq), F32),
                      jax.ShapeDtypeStruct((b, IDX_DIM, seq), F32), jax.ShapeDtypeStruct((b, ATT_W, seq), BF16),
                      jax.ShapeDtypeStruct((b, 2 * IDX_DIM, seq), BF16)]
        out_specs += [col(ATT_W), col(ATT_W), col(IDX_DIM), col(ATT_W), col(2 * IDX_DIM)]
    else:
        out_shape += [jax.ShapeDtypeStruct((n, ATT_W), F32), jax.ShapeDtypeStruct((n, ATT_W), F32)]
        out_specs += [row(ATT_W), row(ATT_W)]
    return pl.pallas_call(
        functools.partial(_inproj_kernel, transposed=transposed),
        grid=(n // tm,),
        in_specs=[row(D_MODEL), _const_spec((1, D_MODEL)), _const_spec(wm.shape), _const_spec(ws.shape),
                  _const_spec(wg.shape), _const_spec((1, LANES)), _const_spec((1, LANES))],
        out_specs=out_specs,
        out_shape=out_shape,
        compiler_params=pltpu.CompilerParams(dimension_semantics=("arbitrary",),
                                             vmem_limit_bytes=VMEM_LIMIT),
        name="inproj",
    )(x2, norm_g, wm, ws, wg, kng, knb)


def _hgrn_tables(c):
    levels = int(math.log2(c))
    assert 2 ** levels == c
    t = np.arange(c)[:, None]
    u = np.arange(c)[None, :]
    sums = [(u <= t), (u > t)]
    masks = [np.eye(c, dtype=bool)]
    for lv in range(levels):
        m = 2 ** lv
        mid = (t // (2 * m)) * (2 * m) + m
        upper = (t % (2 * m)) >= m
        sums.append(upper & (u >= mid) & (u <= t))
        sums.append((~upper) & (u > t) & (u < mid))
        same = (t // (2 * m)) == (u // (2 * m))
        masks.append(same & upper & ((u % (2 * m)) < m))
    return (np.concatenate(sums, 0).astype(np.float32), np.stack(masks).astype(np.float32), levels)


def _hgrn_kernel(h4_ref, s0_ref, lbraw_ref, ong_ref, sums_ref, masks_ref, o_ref, sfin_ref, st_ref,
                 *, c, valid, levels, layer):
    ci = pl.program_id(1)

    @pl.when(ci == 0)
    def _():
        for h in range(HG_HEADS):
            st_ref[h] = s0_ref[0, h].T

    lbraw = lbraw_ref[...]
    e = jnp.exp(lbraw - jnp.max(lbraw, axis=0, keepdims=True))
    sm = e / jnp.sum(e, axis=0, keepdims=True)
    lb = jnp.sum(sm[:layer + 1], axis=0, keepdims=True)

    q = h4_ref[:, 0:HG_W]
    f = lb + (1.0 - lb) * _sigmoid(h4_ref[:, HG_W:2 * HG_W])
    logf = jnp.log(f)
    kk = 1.0 - f
    if valid < c:
        live = lax.broadcasted_iota(jnp.int32, f.shape, 0) < valid
        logf = jnp.where(live, logf, 0.0)
        kk = jnp.where(live, kk, 0.0)
    v = h4_ref[:, 2 * HG_W:3 * HG_W]

    p0 = logf.astype(BF16)
    r0 = logf - p0.astype(F32)
    p1 = r0.astype(BF16)
    p2 = (r0 - p1.astype(F32)).astype(BF16)
    sums = sums_ref[...]
    ex = (jnp.dot(sums, p0, preferred_element_type=F32) + jnp.dot(sums, p1, preferred_element_type=F32)
          + jnp.dot(sums, p2, preferred_element_type=F32))
    nt = (((1,), (1,)), ((), ()))
    tn = (((0,), (0,)), ((), ()))
    for h in range(HG_HEADS):
        hs = slice(h * HG_DK, (h + 1) * HG_DK)
        qh, kh, vh = q[:, hs], kk[:, hs], v[:, hs].astype(BF16)
        a = masks_ref[0] * lax.dot_general(qh.astype(BF16), kh.astype(BF16), nt, preferred_element_type=F32)
        for lv in range(levels):
            eq = ex[(2 + 2 * lv) * c:(3 + 2 * lv) * c, hs]
            ek = ex[(3 + 2 * lv) * c:(4 + 2 * lv) * c, hs]
            ql = (qh * jnp.exp(eq)).astype(BF16)
            kl = (kh * jnp.exp(ek)).astype(BF16)
            a = a + masks_ref[lv + 1] * lax.dot_general(ql, kl, nt, preferred_element_type=F32)
        st = st_ref[h]
        qi = (qh * jnp.exp(ex[0:c, hs])).astype(BF16)
        o = (jnp.dot(a.astype(BF16), vh, preferred_element_type=F32)
             + lax.dot_general(qi, st.astype(BF16), nt, preferred_element_type=F32))
        ke = (kh * jnp.exp(ex[c:2 * c, hs])).astype(BF16)
        st_ref[h] = st * jnp.exp(ex[c - 1:c, hs]) + lax.dot_general(vh, ke, tn, preferred_element_type=F32)
        on = _rms(o, ong_ref[...])
        g = h4_ref[:, 3 * HG_W + h * HG_DV:3 * HG_W + (h + 1) * HG_DV]
        o_ref[:, hs] = (on * (g * _sigmoid(g))).astype(BF16)

    @pl.when(ci == pl.num_programs(1) - 1)
    def _():
        for h in range(HG_HEADS):
            sfin_ref[0, h] = st_ref[h].T


def _hgrn(h4, s0, lbraw, ong, *, seq, chunk, valid, layer):
    n = h4.shape[0]
    b = n // seq
    c = chunk
    nc = seq // c
    assert seq % c == 0 and (valid == c or nc == 1)
    sums, masks, levels = _hgrn_tables(c)
    sums = jnp.asarray(sums, BF16)
    masks = jnp.asarray(masks, F32)
    return pl.pallas_call(
        functools.partial(_hgrn_kernel, c=c, valid=valid, levels=levels, layer=layer),
        grid=(b, nc),
        in_specs=[pl.BlockSpec((c, 4 * HG_W), lambda i, j: (i * nc + j, 0)),
                  pl.BlockSpec((1, HG_HEADS, HG_DK, HG_DV), lambda i, j: (i, 0, 0, 0)),
                  _const_spec(lbraw.shape), _const_spec((1, HG_DV)), _const_spec(sums.shape),
                  _const_spec(masks.shape)],
        out_specs=[pl.BlockSpec((c, HG_W), lambda i, j: (i * nc + j, 0)),
                   pl.BlockSpec((1, HG_HEADS, HG_DK, HG_DV), lambda i, j: (i, 0, 0, 0))],
        out_shape=[jax.ShapeDtypeStruct((n, HG_W), BF16),
                   jax.ShapeDtypeStruct((b, HG_HEADS, HG_DK, HG_DV), F32)],
        scratch_shapes=[pltpu.VMEM((HG_HEADS, HG_DV, HG_DK), F32)],
        compiler_params=pltpu.CompilerParams(dimension_semantics=("arbitrary", "arbitrary"),
                                             vmem_limit_bytes=VMEM_LIMIT),
        name="hgrn",
    )(h4, s0, lbraw, ong, sums, masks)


_KEY_NEG_INF = int(np.array([-np.inf], np.float32).view(np.int32)[0]) ^ 0x7FFFFFFF
_INT_MAX = 2 ** 31 - 1


def _sort_key(s):
    s = jnp.where(s == 0.0, 0.0, s)
    bits = pltpu.bitcast(s, jnp.int32)
    return jnp.where(bits < 0, bits ^ 0x7FFFFFFF, bits)


def _fold_lanes(x):
    acc = x[:, :LANES]
    for j in range(1, x.shape[1] // LANES):
        acc = acc + x[:, j * LANES:(j + 1) * LANES]
    return acc


def _tile_lanes(x, n):
    return x if n == 1 else jnp.concatenate([x] * n, axis=1)


def _topk_threshold(key_ref, t_ref, n_ref, cand_ref, acc_ref, *, nrows, rg, ntiles, tk, topk, idx_bits):
    ones = jnp.ones((LANES, LANES), BF16)
    lane = lax.broadcasted_iota(jnp.int32, (rg, LANES), 1)
    groups = [slice(r * rg, (r + 1) * rg) for r in range(nrows // rg)]

    def for_tiles(rows, body, init):
        def step(kt, carry):
            for j in range(tk // LANES):
                c0 = pl.multiple_of(kt * tk + j * LANES, LANES)
                carry = body(rows, c0, carry)
            return carry
        return lax.fori_loop(0, ntiles, step, init)

    def count(indicator):
        for rows in groups:
            acc_ref[rows, :] = for_tiles(
                rows, lambda rows, c0, acc: acc + indicator(key_ref[rows, pl.ds(c0, LANES)], c0, rows),
                jnp.zeros((rg, LANES), F32))
        return jnp.dot(acc_ref[...].astype(BF16), ones, preferred_element_type=F32)

    t_ref[...] = jnp.full((nrows, LANES), INT_MIN, jnp.int32)
    n_ref[...] = jnp.full((nrows, LANES), ntiles * tk, jnp.int32)

    def bit_step(i, carry):
        cand_ref[...] = t_ref[...] + jnp.left_shift(jnp.int32(1), 31 - i)
        c = count(lambda kk, c0, rows: jnp.where(kk >= cand_ref[rows, :], 1.0, 0.0))
        ok = c >= topk
        t_ref[...] = jnp.where(ok, cand_ref[...], t_ref[...])
        n_ref[...] = jnp.where(ok, c.astype(jnp.int32), n_ref[...])
        return carry
    lax.fori_loop(0, 32, bit_step, 0)

    t = t_ref[...]
    is_inf = t == _KEY_NEG_INF
    has_cut = jnp.where(n_ref[...] > topk, jnp.where(is_inf, 0, 1), 0)

    @pl.when(jnp.max(has_cut) > 0)
    def _():
        n_gt = count(lambda kk, c0, rows: jnp.where(kk > t_ref[rows, :], 1.0, 0.0))
        n_ref[...] = topk - n_gt.astype(jnp.int32)
        cand_ref[...] = jnp.zeros((nrows, LANES), jnp.int32)

        def idx_step(i, carry):
            step = jnp.left_shift(jnp.int32(1), idx_bits - 1 - i)
            below = count(lambda kk, c0, rows: jnp.where(
                kk == t_ref[rows, :], jnp.where(c0 + lane < cand_ref[rows, :] + step, 1.0, 0.0), 0.0))
            cand_ref[...] = jnp.where(below.astype(jnp.int32) < n_ref[...], cand_ref[...] + step, cand_ref[...])
            return carry
        lax.fori_loop(0, idx_bits, idx_step, 0)
        cand_ref[...] = jnp.where(has_cut > 0, cand_ref[...], _INT_MAX)

        def drop(rows, c0, carry):
            kk = key_ref[rows, pl.ds(c0, LANES)]
            late = jnp.where(kk == t_ref[rows, :], jnp.where(c0 + lane > cand_ref[rows, :], 1, 0), 0)
            key_ref[rows, pl.ds(c0, LANES)] = jnp.where(late > 0, INT_MIN, kk)
            return carry
        for rows in groups:
            for_tiles(rows, drop, 0)

    t_ref[...] = jnp.where(is_inf, t + 1, t)


def _pattn_kernel(q_ref, iq_ref, small_ref, kT_ref, vb_ref, kiT_ref, o_ref,
                  qm_scr, iqm_scr, key_scr, t_scr, n_scr, cand_scr, cnt_scr, m_scr, alpha_scr, acc_scr,
                  lga_scr, lgb_scr, p_scr,
                  *, tq, tk, rg, rgs, topk, idx_bits):
    qb = pl.program_id(1)
    q0 = qb * tq
    ntiles = (q0 + tq) // tk
    lane = lax.broadcasted_iota(jnp.int32, (tq, LANES), 1)
    for h in range(ATT_HEADS):
        ps = slice((h // 2) * LANES, (h // 2 + 1) * LANES)
        mine = (lane < ATT_HD) if h % 2 == 0 else (lane >= ATT_HD)
        qm_scr[h] = jnp.where(mine, q_ref[:, ps] * (ATT_HD ** -0.5), 0.0).astype(BF16)
        iqm_scr[h] = jnp.where(mine, iq_ref[:, ps], 0.0).astype(BF16)
    qpos = q0 + lax.broadcasted_iota(jnp.int32, (tq, 1), 0)

    def score_tile(kt, carry):
        k0 = pl.multiple_of(kt * tk, tk)
        kit = kiT_ref[0, :, pl.ds(k0, tk)]
        acc = jnp.zeros((tq, tk), F32)
        for h in range(IDX_HEADS):
            d = jnp.dot(iqm_scr[h], kit, preferred_element_type=F32)
            acc = acc + small_ref[:, IDX_DIM + h:IDX_DIM + h + 1] * jnp.maximum(d, 0.0)
        kpos = k0 + lax.broadcasted_iota(jnp.int32, (tq, tk), 1)
        key_scr[:, pl.ds(k0, tk)] = _sort_key(jnp.where(kpos <= qpos, acc, -jnp.inf))
        return carry
    lax.fori_loop(0, ntiles, score_tile, 0)

    @pl.when(ntiles % 2 == 1)
    def _():
        key_scr[:, pl.ds(pl.multiple_of(ntiles * tk, tk), tk)] = jnp.full((tq, tk), _KEY_NEG_INF, jnp.int32)
    _topk_threshold(key_scr, t_scr, n_scr, cand_scr, cnt_scr, nrows=tq, rg=rgs, ntiles=(ntiles + 1) // 2,
                    tk=2 * tk, topk=topk, idx_bits=idx_bits)

    m_scr[...] = jnp.full(m_scr.shape, NEG_BIG, F32)
    acc_scr[...] = jnp.zeros(acc_scr.shape, F32)
    nl = tk // LANES
    vlane = lax.broadcasted_iota(jnp.int32, (tk, LANES), 1)

    def logits(kt, lg_ref):
        k0 = pl.multiple_of(jnp.minimum(kt, ntiles - 1) * tk, tk)
        for h in range(ATT_HEADS):
            ps = slice((h // 2) * LANES, (h // 2 + 1) * LANES)
            lg_ref[h] = jnp.dot(qm_scr[h], kT_ref[0, ps, pl.ds(k0, tk)], preferred_element_type=F32)

    def attend(kt, lg_ref, half):
        live = kt < ntiles
        k0 = pl.multiple_of(jnp.minimum(kt, ntiles - 1) * tk, tk)
        for r in range(tq // rg):
            rows = slice(r * rg, (r + 1) * rg)
            thr = jnp.where(live, t_scr[rows, :], _INT_MAX)
            bias = jnp.where(key_scr[rows, pl.ds(k0, tk)] >= _tile_lanes(thr, nl), 0.0, NEG_BIG)
            for h in range(ATT_HEADS):
                lg = lg_ref[h, rows, :] + bias
                m_old = m_scr[h, rows, :]
                m_new = jnp.maximum(m_old, jnp.max(lg, axis=-1, keepdims=True))
                alpha_scr[half, h, rows, :] = jnp.exp(m_old - m_new)
                p_scr[half, h, rows, :] = jnp.exp(lg - _tile_lanes(m_new, nl)).astype(BF16)
                m_scr[h, rows, :] = m_new
        for h in range(ATT_HEADS):
            ps = slice((h // 2) * LANES, (h // 2 + 1) * LANES)
            mine = (vlane < ATT_HD) if h % 2 == 0 else (vlane >= ATT_HD)
            w = jnp.where(mine, vb_ref[pl.ds(k0, tk), ps], 1.0)
            acc_scr[h] = (alpha_scr[half, h] * acc_scr[h]
                          + jnp.dot(p_scr[half, h], w, preferred_element_type=F32))

    def attend_pair(i, carry):
        logits(2 * i + 1, lgb_scr)
        attend(2 * i, lga_scr, 0)
        logits(2 * i + 2, lga_scr)
        attend(2 * i + 1, lgb_scr, 1)
        return carry
    logits(0, lga_scr)
    lax.fori_loop(0, (ntiles + 1) // 2, attend_pair, 0)

    for hp in range(ATT_HEADS // 2):
        even, odd = acc_scr[2 * hp], acc_scr[2 * hp + 1]
        even = even / pltpu.roll(even, ATT_HD, 1)
        odd = odd / pltpu.roll(odd, ATT_HD, 1)
        o_ref[:, hp * LANES:(hp + 1) * LANES] = jnp.where(lane < ATT_HD, even, odd).astype(BF16)


def _pattn(aq, iq, small, kT, vb, kiT, *, seq, tq, tk, rg):
    n = aq.shape[0]
    b = n // seq
    topk = min(TOPK_MAX, seq // TOPK_FRAC)
    assert seq % tq == 0 and tq % tk == 0 and tk >= topk and tk % LANES == 0 and tq % rg == 0
    assert seq // LANES < 256
    assert (seq // tk) % 2 == 0
    nq = seq // tq
    row = lambda w: pl.BlockSpec((tq, w), lambda i, j: (i * nq + j, 0))
    return pl.pallas_call(
        functools.partial(_pattn_kernel, tq=tq, tk=tk, rg=rg, rgs=min(tq, 128), topk=topk,
                          idx_bits=(seq - 1).bit_length()),
        grid=(b, nq),
        in_specs=[row(ATT_W), row(IDX_W), row(LANES),
                  pl.BlockSpec((1, ATT_W, seq), lambda i, j: (i, 0, 0)),
                  pl.BlockSpec((seq, ATT_W), lambda i, j: (i, 0)),
                  pl.BlockSpec((1, 2 * IDX_DIM, seq), lambda i, j: (i, 0, 0))],
        out_specs=row(ATT_W),
        out_shape=jax.ShapeDtypeStruct((n, ATT_W), BF16),
        scratch_shapes=[pltpu.VMEM((ATT_HEADS, tq, LANES), BF16), pltpu.VMEM((IDX_HEADS, tq, LANES), BF16),
                        pltpu.VMEM((tq, seq), jnp.int32),
                        pltpu.VMEM((tq, LANES), jnp.int32), pltpu.VMEM((tq, LANES), jnp.int32),
                        pltpu.VMEM((tq, LANES), jnp.int32), pltpu.VMEM((tq, LANES), F32),
                        pltpu.VMEM((ATT_HEADS, tq, LANES), F32), pltpu.VMEM((2, ATT_HEADS, tq, LANES), F32),
                        pltpu.VMEM((ATT_HEADS, tq, LANES), F32),
                        pltpu.VMEM((ATT_HEADS, tq, tk), F32), pltpu.VMEM((ATT_HEADS, tq, tk), F32),
                        pltpu.VMEM((2, ATT_HEADS, tq, tk), BF16)],
        compiler_params=pltpu.CompilerParams(dimension_semantics=("arbitrary", "arbitrary"),
                                             vmem_limit_bytes=VMEM_LIMIT),
        name="prompt_attn",
    )(aq, iq, small, kT, vb, kiT)


def _sscore_kernel(pt_ref, iq_ref, w_ref, kinew_ref, *rest, pp, tnew):
    page_refs, (keys_ref, keysnew_ref) = rest[:pp], rest[pp:]
    j = pl.program_id(1)
    iq = iq_ref[0]
    w = w_ref[0]

    def scores(keys_t):
        d = jnp.dot(iq, keys_t, preferred_element_type=F32)
        r = w * jnp.maximum(d, 0.0)
        acc = jnp.zeros((tnew, LANES), F32)
        for h in range(IDX_HEADS):
            acc = acc + r[h * tnew:(h + 1) * tnew]
        return acc

    for i in range(pp):
        keys_ref[:, i * PAGE_SIZE:(i + 1) * PAGE_SIZE] = _sort_key(scores(page_refs[i][...].astype(BF16)))

    @pl.when(j == pl.num_programs(1) - 1)
    def _():
        s = scores(kinew_ref[0])
        qrow = lax.broadcasted_iota(jnp.int32, (tnew, LANES), 0)
        kcol = lax.broadcasted_iota(jnp.int32, (tnew, LANES), 1)
        keysnew_ref[...] = _sort_key(jnp.where(kcol <= qrow, s, -jnp.inf))


def _sscore(page_table, iq_s, w_s, kinew_t, cache_kidx_t, *, layer, pp):
    b, n_pages = page_table.shape
    tnew = iq_s.shape[1] // IDX_HEADS
    past = n_pages * PAGE_SIZE
    assert n_pages % pp == 0 and tnew <= LANES and tnew % 8 == 0
    page_specs = [pl.BlockSpec((None, None, IDX_DIM, PAGE_SIZE),
                               lambda i, j, pt, s=s: (layer, pt[i, j * pp + s], 0, 0)) for s in range(pp)]
    grid_spec = pltpu.PrefetchScalarGridSpec(
        num_scalar_prefetch=1,
        grid=(b, n_pages // pp),
        in_specs=[pl.BlockSpec((1, IDX_HEADS * tnew, IDX_DIM), lambda i, j, pt: (i, 0, 0)),
                  pl.BlockSpec((1, IDX_HEADS * tnew, 1), lambda i, j, pt: (i, 0, 0)),
                  pl.BlockSpec((1, IDX_DIM, LANES), lambda i, j, pt: (i, 0, 0))] + page_specs,
        out_specs=[pl.BlockSpec((tnew, pp * PAGE_SIZE), lambda i, j, pt: (i, j)),
                   pl.BlockSpec((tnew, LANES), lambda i, j, pt: (i, 0))])
    return pl.pallas_call(
        functools.partial(_sscore_kernel, pp=pp, tnew=tnew),
        grid_spec=grid_spec,
        out_shape=[jax.ShapeDtypeStruct((b * tnew, past), jnp.int32),
                   jax.ShapeDtypeStruct((b * tnew, LANES), jnp.int32)],
        compiler_params=pltpu.CompilerParams(dimension_semantics=("arbitrary", "arbitrary"),
                                             vmem_limit_bytes=VMEM_LIMIT),
        name="sample_scores",
    )(page_table, iq_s, w_s, kinew_t, *([cache_kidx_t] * pp))


def _stopk_kernel(keys_ref, keysnew_ref, bias_ref, key_scr, t_scr, n_scr, cand_scr, cnt_scr,
                  *, rows, past, ntiles, tk, topk, idx_bits):
    key_scr[:, :past] = keys_ref[...]
    key_scr[:, past:] = keysnew_ref[...]
    _topk_threshold(key_scr, t_scr, n_scr, cand_scr, cnt_scr, nrows=rows, rg=rows, ntiles=ntiles, tk=tk,
                    topk=topk, idx_bits=idx_bits)
    t = t_scr[...]
    for c0 in range(0, past + LANES, LANES):
        bias_ref[:, c0:c0 + LANES] = jnp.where(key_scr[:, c0:c0 + LANES] >= t, 0.0, NEG_BIG)


def _stopk(keys, keysnew, *, tnew, group):
    n, past = keys.shape
    total = past + LANES
    rows = group * tnew
    topk = min(TOPK_MAX, (past + tnew) // TOPK_FRAC)
    nl = total // LANES
    per = max(d for d in range(1, 17) if nl % d == 0)
    assert n % rows == 0 and nl < 256
    return pl.pallas_call(
        functools.partial(_stopk_kernel, rows=rows, past=past, ntiles=nl // per, tk=per * LANES, topk=topk,
                          idx_bits=(total - 1).bit_length()),
        grid=(n // rows,),
        in_specs=[pl.BlockSpec((rows, past), lambda i: (i, 0)), pl.BlockSpec((rows, LANES), lambda i: (i, 0))],
        out_specs=pl.BlockSpec((rows, total), lambda i: (i, 0)),
        out_shape=jax.ShapeDtypeStruct((n, total), F32),
        scratch_shapes=[pltpu.VMEM((rows, total), jnp.int32), pltpu.VMEM((rows, LANES), jnp.int32),
                        pltpu.VMEM((rows, LANES), jnp.int32), pltpu.VMEM((rows, LANES), jnp.int32),
                        pltpu.VMEM((rows, LANES), F32)],
        compiler_params=pltpu.CompilerParams(dimension_semantics=("arbitrary",), vmem_limit_bytes=VMEM_LIMIT),
        name="sample_topk",
    )(keys, keysnew)


def _sattn_kernel(pt_ref, q_ref, bias_ref, tail_ref, knew_ref, vnew_ref, *rest, pp, tnew):
    k_refs, v_refs = rest[:pp], rest[pp:2 * pp]
    o_ref, m_scr, l_scr, acc_scr = rest[2 * pp:]
    j = pl.program_id(1)
    nt = (((1,), (1,)), ((), ()))

    @pl.when(j == 0)
    def _():
        m_scr[...] = jnp.full(m_scr.shape, NEG_BIG, F32)
        l_scr[...] = jnp.zeros(l_scr.shape, F32)
        acc_scr[...] = jnp.zeros(acc_scr.shape, F32)

    def update(keys_of, values_of, bias):
        lgs = [jnp.dot(q_ref[0, h], keys_of(h), preferred_element_type=F32) + bias
               for h in range(ATT_HEADS)]
        ps = []
        for h, lg in enumerate(lgs):
            m_old = m_scr[h]
            m_new = jnp.maximum(m_old, jnp.max(lg, axis=-1, keepdims=True))
            alpha = jnp.exp(m_old - m_new)
            p = jnp.exp(lg - _tile_lanes(m_new, lg.shape[1] // LANES))
            l_scr[h] = alpha * l_scr[h] + jnp.sum(p, axis=-1, keepdims=True)
            m_scr[h] = m_new
            ps.append((alpha[:, :ATT_HD], p.astype(BF16)))
        for h, (alpha, p) in enumerate(ps):
            acc_scr[h] = alpha * acc_scr[h] + lax.dot_general(p, values_of(h), nt, preferred_element_type=F32)

    def head_cols(refs, h):
        return jnp.concatenate([r[h] for r in refs], axis=1).astype(BF16)

    update(lambda h: head_cols(k_refs, h), lambda h: head_cols(v_refs, h), bias_ref[...])

    @pl.when(j == pl.num_programs(1) - 1)
    def _():
        update(lambda h: knew_ref[0, h], lambda h: vnew_ref[0, h], tail_ref[...])
        o_ref[0] = jnp.concatenate([acc_scr[h] / l_scr[h][:, :ATT_HD] for h in range(ATT_HEADS)],
                                   axis=-1).astype(BF16)


def _sattn(page_table, q_hm, bias, knew_t, vnew_t, cache_k_t, cache_v_t, *, layer, pp):
    b, n_pages = page_table.shape
    tnew = q_hm.shape[2]
    kv_specs = [pl.BlockSpec((None, None, ATT_HEADS, ATT_HD, PAGE_SIZE),
                             lambda i, j, pt, s=s: (layer, pt[i, j * pp + s], 0, 0, 0)) for s in range(pp)]
    grid_spec = pltpu.PrefetchScalarGridSpec(
        num_scalar_prefetch=1,
        grid=(b, n_pages // pp),
        in_specs=[pl.BlockSpec((1, ATT_HEADS, tnew, ATT_HD), lambda i, j, pt: (i, 0, 0, 0)),
                  pl.BlockSpec((tnew, pp * PAGE_SIZE), lambda i, j, pt: (i, j)),
                  pl.BlockSpec((tnew, LANES), lambda i, j, pt: (i, n_pages)),
                  pl.BlockSpec((1, ATT_HEADS, ATT_HD, LANES), lambda i, j, pt: (i, 0, 0, 0)),
                  pl.BlockSpec((1, ATT_HEADS, ATT_HD, LANES), lambda i, j, pt: (i, 0, 0, 0))] + kv_specs + kv_specs,
        out_specs=pl.BlockSpec((1, tnew, ATT_W), lambda i, j, pt: (i, 0, 0)),
        scratch_shapes=[pltpu.VMEM((ATT_HEADS, tnew, LANES), F32), pltpu.VMEM((ATT_HEADS, tnew, LANES), F32),
                        pltpu.VMEM((ATT_HEADS, tnew, ATT_HD), F32)])
    return pl.pallas_call(
        functools.partial(_sattn_kernel, pp=pp, tnew=tnew),
        grid_spec=grid_spec,
        out_shape=jax.ShapeDtypeStruct((b, tnew, ATT_W), BF16),
        compiler_params=pltpu.CompilerParams(dimension_semantics=("arbitrary", "arbitrary"),
                                             vmem_limit_bytes=VMEM_LIMIT),
        name="sample_attn",
    )(page_table, q_hm, bias, bias, knew_t, vnew_t, *([cache_k_t] * pp), *([cache_v_t] * pp))


def _mixffn_kernel(x_ref, oa_ref, ob_ref, gate_ref, wb_ref, wo_ref, g2_ref, wfi_ref, wfo_ref, gf_ref, y_ref,
                   *, d_ff, ff_chunk, final):
    ga = gate_ref[:, :D_MODEL].astype(F32)
    gb = gate_ref[:, D_MODEL:].astype(F32)
    merged = (jnp.dot(oa_ref[...], wb_ref[0], preferred_element_type=F32) * ga
              + jnp.dot(ob_ref[...], wb_ref[1], preferred_element_type=F32) * gb)
    x1 = x_ref[...] + jnp.dot(merged.astype(BF16), wo_ref[...], preferred_element_type=F32)
    xn = _rms(x1, g2_ref[...]).astype(BF16)
    acc = x1
    for c0 in range(0, d_ff, ff_chunk):
        a = jnp.dot(xn, wfi_ref[:, c0:c0 + ff_chunk], preferred_element_type=F32)
        bgate = jnp.dot(xn, wfi_ref[:, d_ff + c0:d_ff + c0 + ff_chunk], preferred_element_type=F32)
        hact = (a * _sigmoid(a) * bgate).astype(BF16)
        acc = acc + jnp.dot(hact, wfo_ref[c0:c0 + ff_chunk, :], preferred_element_type=F32)
    y_ref[...] = _rms(acc, gf_ref[...]) if final else acc


def _mixffn(x2, oa, ob, gate, wb, wo, g2, wfi, wfo, gf, *, tm, final):
    n = x2.shape[0]
    d_ff = wfo.shape[0]
    ff_chunk = 2 * LANES if d_ff % (2 * LANES) == 0 else d_ff
    row = lambda w: pl.BlockSpec((tm, w), lambda i: (i, 0))
    return pl.pallas_call(
        functools.partial(_mixffn_kernel, d_ff=d_ff, ff_chunk=ff_chunk, final=final),
        grid=(n // tm,),
        in_specs=[row(D_MODEL), row(HG_W), row(ATT_W), row(_GATE_W), _const_spec(wb.shape), _const_spec(wo.shape),
                  _const_spec((1, D_MODEL)), _const_spec(wfi.shape), _const_spec(wfo.shape),
                  _const_spec((1, D_MODEL))],
        out_specs=row(D_MODEL),
        out_shape=jax.ShapeDtypeStruct((n, D_MODEL), F32),
        compiler_params=pltpu.CompilerParams(dimension_semantics=("arbitrary",),
                                             vmem_limit_bytes=VMEM_LIMIT),
        name="mix_ffn",
    )(x2, oa, ob, gate, wb, wo, g2, wfi, wfo, gf)


def _pick_tile(n, pref):
    t = min(n, pref)
    while n % t:
        t //= 2
    return t


def kernel(x_prompt, x_sample, cache_k, cache_v, cache_kidx, state_hgrn, page_table, norm1_g, w_in, hg_lb_raw,
           hg_onorm_g, idx_knorm_g, idx_knorm_b, w_branch, w_out, norm2_g, w_ffn_in, w_ffn_out, final_g):
    bp, tp, _ = x_prompt.shape
    bs, ts, _ = x_sample.shape
    depth = w_in.shape[0]
    n_pool = cache_k.shape[1]
    xp = x_prompt.reshape(bp * tp, D_MODEL)
    xs = x_sample.reshape(bs * ts, D_MODEL)
    row = lambda a: a.reshape(1, -1)
    pad_lanes = lambda a: jnp.pad(a, (0, LANES - a.shape[0])).reshape(1, LANES)
    outs = [[] for _ in range(8)]
    for l in range(depth):
        wm = w_in[l, :, :_MAIN_W].astype(BF16)
        ws = jnp.pad(w_in[l, :, _MAIN_W:_MAIN_W + _SMALL_W], ((0, 0), (0, LANES - _SMALL_W))).astype(BF16)
        wg = w_in[l, :, _MAIN_W + _SMALL_W:].astype(BF16)
        kng, knb = pad_lanes(idx_knorm_g[l]), pad_lanes(idx_knorm_b[l])
        proj = functools.partial(_inproj, norm_g=row(norm1_g[l]), wm=wm, ws=ws, wg=wg, kng=kng, knb=knb)
        mix = functools.partial(_mixffn, wb=w_branch[l].astype(BF16), wo=w_out[l].astype(BF16), g2=row(norm2_g[l]),
                                wfi=w_ffn_in[l].astype(BF16), wfo=w_ffn_out[l].astype(BF16), gf=row(final_g),
                                final=(l == depth - 1))
        hgrn = functools.partial(_hgrn, lbraw=hg_lb_raw, ong=row(hg_onorm_g[l]), layer=l)

        tm = _pick_tile(tp, 512)
        h4, aq, vb, iq, small, gate, kt, vt, kit, ktb, kitb = proj(xp, seq=tp, tm=tm, transposed=True)
        cp = math.gcd(tp, HG_CHUNK)
        oa, s_p = hgrn(h4, jnp.zeros((bp, HG_HEADS, HG_DK, HG_DV), F32), seq=tp, chunk=cp, valid=cp)
        tq = _pick_tile(tp, 256)
        ob = _pattn(aq, iq, small, ktb, vb, kitb, seq=tp, tq=tq, tk=tq, rg=min(tq, 64))
        xp = mix(xp, oa, ob, gate, tm=tm)
        outs[0].append(kt.reshape(bp, ATT_HEADS, ATT_HD, tp).transpose(0, 3, 1, 2))
        outs[1].append(vt.reshape(bp, ATT_HEADS, ATT_HD, tp).transpose(0, 3, 1, 2))
        outs[2].append(kit.transpose(0, 2, 1))
        outs[3].append(s_p)

        ns = bs * ts
        tms = _pick_tile(ns, 512)
        h4, aq, vb, iq, small, gate, k, v = proj(xs, seq=ts, tm=tms, transposed=False)
        cs = 16
        assert ts <= cs
        h4p = jnp.pad(h4.reshape(bs, ts, 4 * HG_W), ((0, 0), (0, cs - ts), (0, 0))).reshape(bs * cs, 4 * HG_W)
        oa, s_s = hgrn(h4p, state_hgrn[l], seq=cs, chunk=cs, valid=ts)
        oa = oa.reshape(bs, cs, HG_W)[:, :ts].reshape(ns, HG_W)
        iq_s = iq.reshape(bs, ts, IDX_HEADS, IDX_DIM).transpose(0, 2, 1, 3).reshape(bs, IDX_HEADS * ts, IDX_DIM)
        w_s = small[:, IDX_DIM:_SMALL_W].reshape(bs, ts, IDX_HEADS).transpose(0, 2, 1).reshape(bs, IDX_HEADS * ts, 1)
        n_pages = page_table.shape[1]
        pad_cols = lambda a: jnp.pad(a, ((0, 0),) * (a.ndim - 1) + ((0, LANES - ts),)).astype(BF16)
        kinew_t = pad_cols(small[:, :IDX_DIM].reshape(bs, ts, IDX_DIM).transpose(0, 2, 1))
        keys, keysnew = _sscore(page_table, iq_s, w_s, kinew_t, cache_kidx.transpose(0, 1, 3, 2), layer=l,
                                pp=_pick_tile(n_pages, 16))
        bias = _stopk(keys, keysnew, tnew=ts, group=_pick_tile(bs, 8))
        q_hm = (aq * (ATT_HD ** -0.5)).reshape(bs, ts, ATT_HEADS, ATT_HD).transpose(0, 2, 1, 3)
        new_t = lambda a: pad_cols(a.reshape(bs, ts, ATT_HEADS, ATT_HD).transpose(0, 2, 3, 1))
        ob = _sattn(page_table, q_hm, bias, new_t(k), new_t(v), cache_k.transpose(0, 1, 3, 4, 2),
                    cache_v.transpose(0, 1, 3, 4, 2), layer=l, pp=_pick_tile(n_pages, 8)).reshape(ns, ATT_W)
        xs = mix(xs, oa, ob, gate, tm=tms)
        outs[4].append(k.reshape(bs, ts, ATT_HEADS, ATT_HD))
        outs[5].append(v.reshape(bs, ts, ATT_HEADS, ATT_HD))
        outs[6].append(small[:, :IDX_DIM].reshape(bs, ts, IDX_DIM))
        outs[7].append(s_s)
    return (xp.reshape(bp, tp, D_MODEL), xs.reshape(bs, ts, D_MODEL)) + tuple(jnp.stack(o) for o in outs)
```
